```python
import math
import jax, jax.numpy as jnp
from jax import lax
import numpy as np

D_MODEL = 2048
BATCH = 2
SEQ = 8192
DEPTH = 2

CHUNK = 64
Q_BLOCK = 2 * CHUNK

N_A = (DEPTH + 1) // 2
N_B = DEPTH - N_A

D_RNN = 5 * D_MODEL // 4
LRU_BLOCKS = 10
LRU_BW = D_RNN // LRU_BLOCKS
CONV_W = 4
LRU_C = 8.0

N_HEADS = 16
HEAD_DIM = 128
D_ATTN = N_HEADS * HEAD_DIM

EPS = 1e-6

kernel_name = "hawk_stickbreak_yoco_trunk"


def _rmsnorm(x, g):
    xf = x.astype(jnp.float32)
    y = xf * lax.rsqrt(jnp.mean(xf * xf, axis=-1, keepdims=True) + EPS)
    return (y * g.astype(jnp.float32)).astype(x.dtype)


def _causal_depthwise_conv(xb, w, b):
    kernel = w[:, None, :].astype(xb.dtype)
    y = lax.conv_general_dilated(
        xb, kernel, window_strides=(1,), padding=[(CONV_W - 1, 0)],
        dimension_numbers=("NWC", "WIO", "NWC"), feature_group_count=xb.shape[-1])
    return y + b.astype(xb.dtype)


def _block_diag(xb, w, b):
    bsz, s, _ = xb.shape
    xr = xb.reshape(bsz, s, LRU_BLOCKS, LRU_BW)
    y = jnp.einsum("bsnc,ncd->bsnd", xr, w.astype(xb.dtype)).reshape(bsz, s, D_RNN)
    return y + b.astype(xb.dtype)


def _rg_lru(xb, w_r, b_r, w_i, b_i, lam):
    r = jax.nn.sigmoid(_block_diag(xb, w_r, b_r).astype(jnp.float32))
    i = jax.nn.sigmoid(_block_diag(xb, w_i, b_i).astype(jnp.float32))
    log_a = -LRU_C * r * jax.nn.softplus(-lam.astype(jnp.float32))
    a = jnp.exp(log_a)
    mult = jnp.sqrt(jnp.maximum(-jnp.expm1(2.0 * log_a), 0.0))
    u = mult * (i * xb.astype(jnp.float32))

    def combine(left, right):
        a1, b1 = left
        a2, b2 = right
        return a1 * a2, a2 * b1 + b2

    _, h = lax.associative_scan(combine, (a, u), axis=1)
    return h.astype(xb.dtype)


def _stick_breaking_attention(q, k, v):
    bsz, s, h, dh = q.shape
    n_blocks = s // Q_BLOCK
    scale = 1.0 / math.sqrt(dh)
    qb = q.reshape(bsz, n_blocks, Q_BLOCK, h, dh).transpose(1, 0, 2, 3, 4)
    starts = jnp.arange(n_blocks, dtype=jnp.int32) * Q_BLOCK
    key_pos = jnp.arange(s, dtype=jnp.int32)

    def one_block(args):
        start, qblk = args
        z = jnp.einsum("bqhd,bkhd->bhqk", qblk, k).astype(jnp.float32) * scale
        q_pos = start + jnp.arange(Q_BLOCK, dtype=jnp.int32)
        causal = key_pos[None, :] < q_pos[:, None]
        log_keep = jnp.where(causal, jax.nn.log_sigmoid(-z), 0.0)
        later = lax.cumsum(log_keep, axis=3, reverse=True) - log_keep
        weights = jnp.where(causal, jnp.exp(jax.nn.log_sigmoid(z) + later), 0.0)
        return jnp.einsum("bhqk,bkhd->bqhd", weights.astype(v.dtype), v)

    out = lax.map(one_block, (starts, qb))
    return out.transpose(1, 0, 2, 3, 4).reshape(bsz, s, h, dh)


def setup_inputs(seed: int = 0) -> dict:
    key = jax.random.key(seed)
    ks = jax.random.split(key, 20)

    def nrm(k, shape, scale):
        return jax.random.normal(k, shape, jnp.float32) * scale

    x = nrm(ks[0], (BATCH, SEQ, D_MODEL), 1.0)
    a_norm = 1.0 + nrm(ks[1], (N_A, D_MODEL), 0.01)
    a_w_in = nrm(ks[2], (N_A, D_MODEL, 2 * D_RNN), D_MODEL ** -0.5)
    a_conv_w = nrm(ks[3], (N_A, CONV_W, D_RNN), CONV_W ** -0.5)
    a_conv_b = nrm(ks[4], (N_A, D_RNN), 0.01)
    a_w_r = nrm(ks[5], (N_A, LRU_BLOCKS, LRU_BW, LRU_BW), LRU_BW ** -0.5)
    a_b_r = nrm(ks[6], (N_A, D_RNN), 0.01)
    a_w_i = nrm(ks[7], (N_A, LRU_BLOCKS, LRU_BW, LRU_BW), LRU_BW ** -0.5)
    a_b_i = nrm(ks[8], (N_A, D_RNN), 0.01)
    u = jax.random.uniform(ks[9], (N_A, D_RNN), jnp.float32, minval=0.9, maxval=0.999)
    a_root = u ** (1.0 / LRU_C)
    a_lambda = jnp.log(a_root) - jnp.log1p(-a_root)
    a_w_out = nrm(ks[10], (N_A, D_RNN, D_MODEL), D_RNN ** -0.5)
    kv_norm = 1.0 + nrm(ks[11], (D_MODEL,), 0.01)
    w_kv = nrm(ks[12], (D_MODEL, 2 * D_ATTN), D_MODEL ** -0.5)
    b_norm = 1.0 + nrm(ks[13], (N_B, D_MODEL), 0.01)
    b_w_in = nrm(ks[14], (N_B, D_MODEL, 2 * D_ATTN), D_MODEL ** -0.5)
    b_w_out = nrm(ks[15], (N_B, D_ATTN, D_MODEL), D_ATTN ** -0.5)
    final_norm = 1.0 + nrm(ks[16], (D_MODEL,), 0.01)
    return {
        "x": x, "a_norm": a_norm, "a_w_in": a_w_in, "a_conv_w": a_conv_w,
        "a_conv_b": a_conv_b, "a_w_r": a_w_r, "a_b_r": a_b_r, "a_w_i": a_w_i,
        "a_b_i": a_b_i, "a_lambda": a_lambda, "a_w_out": a_w_out,
        "kv_norm": kv_norm, "w_kv": w_kv, "b_norm": b_norm, "b_w_in": b_w_in,
        "b_w_out": b_w_out, "final_norm": final_norm,
    }


def reference(x, a_norm, a_w_in, a_conv_w, a_conv_b, a_w_r, a_b_r, a_w_i, a_b_i,
              a_lambda, a_w_out, kv_norm, w_kv, b_norm, b_w_in, b_w_out, final_norm):
    bsz, s, _ = x.shape
    k = v = None
    for layer in range(DEPTH):
        if layer < N_A:
            h = _rmsnorm(x, a_norm[layer])
            proj = h @ a_w_in[layer].astype(x.dtype)
            xb, gate = proj[..., :D_RNN], proj[..., D_RNN:]
            xb = _causal_depthwise_conv(xb, a_conv_w[layer], a_conv_b[layer])
            y = _rg_lru(xb, a_w_r[layer], a_b_r[layer], a_w_i[layer], a_b_i[layer],
                        a_lambda[layer])
            x = x + (y * jax.nn.silu(gate)) @ a_w_out[layer].astype(x.dtype)
            if layer == N_A - 1:
                kv = _rmsnorm(x, kv_norm) @ w_kv.astype(x.dtype)
                k = kv[..., :D_ATTN].reshape(bsz, s, N_HEADS, HEAD_DIM)
                v = kv[..., D_ATTN:].reshape(bsz, s, N_HEADS, HEAD_DIM)
        else:
            lb = layer - N_A
            h = _rmsnorm(x, b_norm[lb])
            proj = h @ b_w_in[lb].astype(x.dtype)
            q = proj[..., :D_ATTN].reshape(bsz, s, N_HEADS, HEAD_DIM)
            gate = proj[..., D_ATTN:]
            o = _stick_breaking_attention(q, k, v).reshape(bsz, s, D_ATTN)
            x = x + (o * jax.nn.silu(gate)) @ b_w_out[lb].astype(x.dtype)
    return _rmsnorm(x, final_norm)
```

```python
import functools
import math

import jax
import jax.numpy as jnp
from jax import lax
from jax.experimental import pallas as pl
from jax.experimental.pallas import tpu as pltpu

EPS = 1e-6
LRU_C = 8.0
CONV_W = 4
LRU_BW = 256
N_HEADS = 16
HEAD_DIM = 128

SUBLANES = 8

MIB = 1024 * 1024


def _params(semantics, vmem_mib):
    return pltpu.CompilerParams(dimension_semantics=semantics,
                                vmem_limit_bytes=vmem_mib * MIB)


def _silu(g):
    return g * jax.nn.sigmoid(g)


def _norm_matmul_kernel(x_ref, g_ref, w_ref, o_ref, h_ref):
    @pl.when(pl.program_id(1) == 0)
    def _():
        x = x_ref[...]
        ms = jnp.mean(x * x, axis=-1, keepdims=True)
        h_ref[...] = (x * lax.rsqrt(ms + EPS) * g_ref[...]).astype(h_ref.dtype)

    o_ref[...] = jnp.dot(h_ref[...], w_ref[...],
                         preferred_element_type=jnp.float32).astype(o_ref.dtype)


def _norm_matmul(x, g, w, out_dtype, *, tm=512, tn=512, name):
    m, k = x.shape
    n = w.shape[1]
    return pl.pallas_call(
        _norm_matmul_kernel,
        grid=(m // tm, n // tn),
        in_specs=[
            pl.BlockSpec((tm, k), lambda i, j: (i, 0)),
            pl.BlockSpec((1, k), lambda i, j: (0, 0)),
            pl.BlockSpec((k, tn), lambda i, j: (0, j)),
        ],
        out_specs=pl.BlockSpec((tm, tn), lambda i, j: (i, j)),
        out_shape=jax.ShapeDtypeStruct((m, n), out_dtype),
        scratch_shapes=[pltpu.VMEM((tm, k), jnp.bfloat16)],
        compiler_params=_params(("parallel", "arbitrary"), 40),
        name=name,
    )(x, g, w)


def _lru_kernel(xb_ref, gate_ref, cw_ref, cb_ref, wr_ref, br_ref, wi_ref, bi_ref,
                lam_ref, o_ref, xext, a_s, u_s, h_s):
    t = pl.program_id(2)
    tt, cw = xb_ref.shape

    @pl.when(t == 0)
    def _():
        xext[0:SUBLANES, :] = jnp.zeros((SUBLANES, cw), jnp.float32)
        h_s[...] = jnp.zeros_like(h_s)

    @pl.when(t > 0)
    def _():
        xext[0:SUBLANES, :] = xext[tt:tt + SUBLANES, :]

    xext[SUBLANES:SUBLANES + tt, :] = xb_ref[...]

    xc = cb_ref[...] + cw_ref[0:1, :] * xext[SUBLANES - 3:SUBLANES - 3 + tt, :]
    for w in range(1, CONV_W):
        off = SUBLANES - (CONV_W - 1) + w
        xc = xc + cw_ref[w:w + 1, :] * xext[off:off + tt, :]

    xcb = xc.astype(jnp.bfloat16)
    for n in range(cw // LRU_BW):
        sl = slice(n * LRU_BW, (n + 1) * LRU_BW)
        xn = xcb[:, sl]
        r = jax.nn.sigmoid(
            jnp.dot(xn, wr_ref[n], preferred_element_type=jnp.float32) + br_ref[:, sl])
        i = jax.nn.sigmoid(
            jnp.dot(xn, wi_ref[n], preferred_element_type=jnp.float32) + bi_ref[:, sl])
        neg_lam = -lam_ref[:, sl]
        softplus = jnp.maximum(neg_lam, 0.0) + jnp.log1p(jnp.exp(-jnp.abs(neg_lam)))
        log_a = -LRU_C * r * softplus
        a = jnp.exp(log_a)
        one_minus_a2 = -jnp.tanh(log_a) * (a * a + 1.0)
        mult = jnp.sqrt(jnp.maximum(one_minus_a2, 0.0))
        a_s[:, sl] = a
        u_s[:, sl] = mult * (i * xc[:, sl])

    def step(s, h):
        h = a_s[pl.ds(s, 1), :] * h + u_s[pl.ds(s, 1), :]
        u_s[pl.ds(s, 1), :] = h
        return h

    h_s[...] = lax.fori_loop(0, tt, step, h_s[...], unroll=8)
    o_ref[...] = (u_s[...] * _silu(gate_ref[...])).astype(o_ref.dtype)


def _lru(proj, conv_w, conv_b, w_r, b_r, w_i, b_i, lam, *, tt=256, cw=1280):
    bsz, s, two_c = proj.shape
    c = two_c // 2
    nc = c // cw
    nb = cw // LRU_BW
    row = lambda b, ci, t: (0, ci)
    return pl.pallas_call(
        _lru_kernel,
        grid=(bsz, nc, s // tt),
        in_specs=[
            pl.BlockSpec((None, tt, cw), lambda b, ci, t: (b, t, ci)),
            pl.BlockSpec((None, tt, cw), lambda b, ci, t: (b, t, nc + ci)),
            pl.BlockSpec((CONV_W, cw), row),
            pl.BlockSpec((1, cw), row),
            pl.BlockSpec((nb, LRU_BW, LRU_BW), lambda b, ci, t: (ci, 0, 0)),
            pl.BlockSpec((1, cw), row),
            pl.BlockSpec((nb, LRU_BW, LRU_BW), lambda b, ci, t: (ci, 0, 0)),
            pl.BlockSpec((1, cw), row),
            pl.BlockSpec((1, cw), row),
        ],
        out_specs=pl.BlockSpec((None, tt, cw), lambda b, ci, t: (b, t, ci)),
        out_shape=jax.ShapeDtypeStruct((bsz, s, c), jnp.bfloat16),
        scratch_shapes=[
            pltpu.VMEM((tt + SUBLANES, cw), jnp.float32),
            pltpu.VMEM((tt, cw), jnp.float32),
            pltpu.VMEM((tt, cw), jnp.float32),
            pltpu.VMEM((1, cw), jnp.float32),
        ],
        compiler_params=_params(("parallel", "parallel", "arbitrary"), 40),
        name="lru",
    )(proj, proj, conv_w, conv_b, w_r, b_r, w_i, b_i, lam)


def _matmul_residual_kernel(a_ref, w_ref, r_ref, o_ref):
    o_ref[...] = r_ref[...] + jnp.dot(a_ref[...], w_ref[...],
                                      preferred_element_type=jnp.float32)


def _matmul_residual(a, w, res, *, tm=512, tn=512):
    m, k = a.shape
    n = w.shape[1]
    return pl.pallas_call(
        _matmul_residual_kernel,
        grid=(m // tm, n // tn),
        in_specs=[
            pl.BlockSpec((tm, k), lambda i, j: (i, 0)),
            pl.BlockSpec((k, tn), lambda i, j: (0, j)),
            pl.BlockSpec((tm, tn), lambda i, j: (i, j)),
        ],
        out_specs=pl.BlockSpec((tm, tn), lambda i, j: (i, j)),
        out_shape=jax.ShapeDtypeStruct((m, n), jnp.float32),
        compiler_params=_params(("parallel", "parallel"), 40),
        name="matmul_residual",
    )(a, w, res)


def _attention_kernel(q_ref, k_ref, v_ref, o_ref, *, scale):
    i = pl.program_id(2)
    tq, dh = q_ref.shape
    tk = tq
    q = q_ref[...].astype(jnp.bfloat16)

    rows = lax.broadcasted_iota(jnp.int32, (tk, tk), 0)
    cols = lax.broadcasted_iota(jnp.int32, (tk, tk), 1)
    upper = (rows > cols).astype(jnp.bfloat16)
    causal = cols < rows

    def block(kb, vb, c, acc, masked):
        z = lax.dot_general(q, kb, (((1,), (1,)), ((), ())),
                            preferred_element_type=jnp.float32) * scale
        l = jnp.log(1.0 + jnp.exp(-jnp.abs(z)))
        sp = jnp.maximum(z, 0.0) + l
        logb = jnp.minimum(z, 0.0) - l
        if masked:
            sp = jnp.where(causal, sp, 0.0)
        hi = sp.astype(jnp.bfloat16)
        lo = (sp - hi.astype(jnp.float32)).astype(jnp.bfloat16)
        later = (jnp.dot(hi, upper, preferred_element_type=jnp.float32)
                 + jnp.dot(lo, upper, preferred_element_type=jnp.float32))
        w = jnp.exp(logb - later - c)
        if masked:
            w = jnp.where(causal, w, 0.0)
        acc = acc + jnp.dot(w.astype(jnp.bfloat16), vb,
                            preferred_element_type=jnp.float32)
        c = c + (later[:, 0:1] + sp[:, 0:1])
        return c, acc

    c0 = jnp.zeros((tq, 1), jnp.float32)
    acc0 = jnp.zeros((tq, dh), jnp.float32)
    d0 = pl.multiple_of(i * tk, tk)
    c, acc = block(k_ref[pl.ds(d0, tk), :], v_ref[pl.ds(d0, tk), :], c0, acc0, True)

    def body(jj, carry):
        c, acc = carry
        start = pl.multiple_of((i - 1 - jj) * tk, tk)
        return block(k_ref[pl.ds(start, tk), :], v_ref[pl.ds(start, tk), :],
                     c, acc, False)

    c, acc = lax.fori_loop(0, i, body, (c, acc))
    o_ref[...] = acc.astype(o_ref.dtype)


def _attention(proj_b, kv, *, tq=256):
    bsz, s, _ = proj_b.shape
    kern = functools.partial(_attention_kernel, scale=1.0 / math.sqrt(HEAD_DIM))
    return pl.pallas_call(
        kern,
        grid=(bsz, N_HEADS, s // tq),
        in_specs=[
            pl.BlockSpec((None, tq, HEAD_DIM), lambda b, h, i: (b, i, h)),
            pl.BlockSpec((None, s, HEAD_DIM), lambda b, h, i: (b, 0, h)),
            pl.BlockSpec((None, s, HEAD_DIM), lambda b, h, i: (b, 0, N_HEADS + h)),
        ],
        out_specs=pl.BlockSpec((None, tq, HEAD_DIM), lambda b, h, i: (b, i, h)),
        out_shape=jax.ShapeDtypeStruct((bsz, s, N_HEADS * HEAD_DIM), jnp.float32),
        compiler_params=_params(("parallel", "parallel", "arbitrary"), 40),
        name="attention",
    )(proj_b, kv, kv)


def _gated_out_norm_kernel(o_ref, g_ref, w_ref, x_ref, fn_ref, out_ref):
    a = (o_ref[...] * _silu(g_ref[...])).astype(jnp.bfloat16)
    y = x_ref[...] + jnp.dot(a, w_ref[...], preferred_element_type=jnp.float32)
    ms = jnp.mean(y * y, axis=-1, keepdims=True)
    out_ref[...] = y * lax.rsqrt(ms + EPS) * fn_ref[...]


def _gated_out_norm(o, proj_b, w, x1, fn, *, tm=256):
    m, d = x1.shape
    return pl.pallas_call(
        _gated_out_norm_kernel,
        grid=(m // tm,),
        in_specs=[
            pl.BlockSpec((tm, d), lambda i: (i, 0)),
            pl.BlockSpec((tm, d), lambda i: (i, 1)),
            pl.BlockSpec((d, d), lambda i: (0, 0)),
            pl.BlockSpec((tm, d), lambda i: (i, 0)),
            pl.BlockSpec((1, d), lambda i: (0, 0)),
        ],
        out_specs=pl.BlockSpec((tm, d), lambda i: (i, 0)),
        out_shape=jax.ShapeDtypeStruct((m, d), jnp.float32),
        compiler_params=_params(("parallel",), 48),
        name="gated_out_norm",
    )(o, proj_b, w, x1, fn)


def kernel(x, a_norm, a_w_in, a_conv_w, a_conv_b, a_w_r, a_b_r, a_w_i, a_b_i, a_lambda,
           a_w_out, kv_norm, w_kv, b_norm, b_w_in, b_w_out, final_norm):
    bsz, s, d = x.shape
    m = bsz * s
    bf = jnp.bfloat16
    x2 = x.reshape(m, d)

    proj_a = _norm_matmul(x2, a_norm[0:1], a_w_in[0].astype(bf), jnp.float32,
                          name="norm_matmul_a")
    c = proj_a.shape[1] // 2
    yg = _lru(proj_a.reshape(bsz, s, 2 * c), a_conv_w[0], a_conv_b[0:1],
              a_w_r[0].astype(bf), a_b_r[0:1], a_w_i[0].astype(bf), a_b_i[0:1],
              a_lambda[0:1])
    x1 = _matmul_residual(yg.reshape(m, c), a_w_out[0].astype(bf), x2)

    kv = _norm_matmul(x1, kv_norm.reshape(1, d), w_kv.astype(bf), bf, name="norm_matmul_kv")
    proj_b = _norm_matmul(x1, b_norm[0:1], b_w_in[0].astype(bf), jnp.float32,
                          name="norm_matmul_b")
    da = proj_b.shape[1] // 2
    o = _attention(proj_b.reshape(bsz, s, 2 * da), kv.reshape(bsz, s, 2 * da))
    out = _gated_out_norm(o.reshape(m, da), proj_b, b_w_out[0].astype(bf), x1,
                          final_norm.reshape(1, d))
    return out.reshape(bsz, s, d)
```

```python
import functools
import math

import jax
import jax.numpy as jnp
from jax import lax
from jax.experimental import pallas as pl
from jax.experimental.pallas import tpu as pltpu

EPS = 1e-6
LOG2E = 1.0 / math.log(2.0)
LRU_C = 8.0
CONV_W = 4
LRU_BW = 256
N_HEADS = 16
HEAD_DIM = 128
UNROLL = 2
SKIP_BITS = 200.0

SUBLANES = 8

MIB = 1024 * 1024


def _params(semantics, vmem_mib):
    return pltpu.CompilerParams(dimension_semantics=semantics,
                                vmem_limit_bytes=vmem_mib * MIB)


def _silu(g):
    return g * jax.nn.sigmoid(g)


def _norm_matmul_kernel(x_ref, g_ref, w_ref, o_ref, h_ref):
    @pl.when(pl.program_id(1) == 0)
    def _():
        x = x_ref[...]
        ms = jnp.mean(x * x, axis=-1, keepdims=True)
        h_ref[...] = (x * lax.rsqrt(ms + EPS) * g_ref[...]).astype(h_ref.dtype)

    o_ref[...] = jnp.dot(h_ref[...], w_ref[...],
                         preferred_element_type=jnp.float32).astype(o_ref.dtype)


def _norm_matmul(x, g, w, out_dtype, *, tm=1024, tn=512, name):
    m, k = x.shape
    n = w.shape[1]
    return pl.pallas_call(
        _norm_matmul_kernel,
        grid=(m // tm, n // tn),
        in_specs=[
            pl.BlockSpec((tm, k), lambda i, j: (i, 0)),
            pl.BlockSpec((1, k), lambda i, j: (0, 0)),
            pl.BlockSpec((k, tn), lambda i, j: (0, j)),
        ],
        out_specs=pl.BlockSpec((tm, tn), lambda i, j: (i, j)),
        out_shape=jax.ShapeDtypeStruct((m, n), out_dtype),
        scratch_shapes=[pltpu.VMEM((tm, k), jnp.bfloat16)],
        compiler_params=_params(("parallel", "arbitrary"), 40),
        name=name,
    )(x, g, w)


def _lru_kernel(xb_ref, gate_ref, cw_ref, cb_ref, wr_ref, br_ref, wi_ref, bi_ref,
                lam_ref, o_ref, xext, a_s, u_s, h_s):
    t = pl.program_id(2)
    tt, cw = xb_ref.shape

    @pl.when(t == 0)
    def _():
        xext[0:SUBLANES, :] = jnp.zeros((SUBLANES, cw), jnp.float32)
        h_s[...] = jnp.zeros_like(h_s)

    @pl.when(t > 0)
    def _():
        xext[0:SUBLANES, :] = xext[tt:tt + SUBLANES, :]

    xext[SUBLANES:SUBLANES + tt, :] = xb_ref[...]

    xc = cb_ref[...] + cw_ref[0:1, :] * xext[SUBLANES - 3:SUBLANES - 3 + tt, :]
    for w in range(1, CONV_W):
        off = SUBLANES - (CONV_W - 1) + w
        xc = xc + cw_ref[w:w + 1, :] * xext[off:off + tt, :]

    xcb = xc.astype(jnp.bfloat16)
    for n in range(cw // LRU_BW):
        sl = slice(n * LRU_BW, (n + 1) * LRU_BW)
        xn = xcb[:, sl]
        r = jax.nn.sigmoid(
            jnp.dot(xn, wr_ref[n], preferred_element_type=jnp.float32) + br_ref[:, sl])
        i = jax.nn.sigmoid(
            jnp.dot(xn, wi_ref[n], preferred_element_type=jnp.float32) + bi_ref[:, sl])
        neg_lam = -lam_ref[:, sl]
        softplus = jnp.maximum(neg_lam, 0.0) + jnp.log1p(jnp.exp(-jnp.abs(neg_lam)))
        log_a = -LRU_C * r * softplus
        a = jnp.exp(log_a)
        one_minus_a2 = -jnp.tanh(log_a) * (a * a + 1.0)
        mult = jnp.sqrt(jnp.maximum(one_minus_a2, 0.0))
        a_s[:, sl] = a
        u_s[:, sl] = mult * (i * xc[:, sl])

    def step(s, h):
        h = a_s[pl.ds(s, 1), :] * h + u_s[pl.ds(s, 1), :]
        u_s[pl.ds(s, 1), :] = h
        return h

    h_s[...] = lax.fori_loop(0, tt, step, h_s[...], unroll=8)
    o_ref[...] = (u_s[...] * _silu(gate_ref[...])).astype(o_ref.dtype)


def _lru(proj, conv_w, conv_b, w_r, b_r, w_i, b_i, lam, *, tt=256, cw=1280):
    bsz, s, two_c = proj.shape
    c = two_c // 2
    nc = c // cw
    nb = cw // LRU_BW
    row = lambda b, ci, t: (0, ci)
    return pl.pallas_call(
        _lru_kernel,
        grid=(bsz, nc, s // tt),
        in_specs=[
            pl.BlockSpec((None, tt, cw), lambda b, ci, t: (b, t, ci)),
            pl.BlockSpec((None, tt, cw), lambda b, ci, t: (b, t, nc + ci)),
            pl.BlockSpec((CONV_W, cw), row),
            pl.BlockSpec((1, cw), row),
            pl.BlockSpec((nb, LRU_BW, LRU_BW), lambda b, ci, t: (ci, 0, 0)),
            pl.BlockSpec((1, cw), row),
            pl.BlockSpec((nb, LRU_BW, LRU_BW), lambda b, ci, t: (ci, 0, 0)),
            pl.BlockSpec((1, cw), row),
            pl.BlockSpec((1, cw), row),
        ],
        out_specs=pl.BlockSpec((None, tt, cw), lambda b, ci, t: (b, t, ci)),
        out_shape=jax.ShapeDtypeStruct((bsz, s, c), jnp.bfloat16),
        scratch_shapes=[
            pltpu.VMEM((tt + SUBLANES, cw), jnp.float32),
            pltpu.VMEM((tt, cw), jnp.float32),
            pltpu.VMEM((tt, cw), jnp.float32),
            pltpu.VMEM((1, cw), jnp.float32),
        ],
        compiler_params=_params(("parallel", "parallel", "arbitrary"), 40),
        name="lru",
    )(proj, proj, conv_w, conv_b, w_r, b_r, w_i, b_i, lam)


def _matmul_residual_kernel(a_ref, w_ref, r_ref, o_ref):
    o_ref[...] = r_ref[...] + jnp.dot(a_ref[...], w_ref[...],
                                      preferred_element_type=jnp.float32)


def _matmul_residual(a, w, res, *, tm=1024, tn=512):
    m, k = a.shape
    n = w.shape[1]
    return pl.pallas_call(
        _matmul_residual_kernel,
        grid=(m // tm, n // tn),
        in_specs=[
            pl.BlockSpec((tm, k), lambda i, j: (i, 0)),
            pl.BlockSpec((k, tn), lambda i, j: (0, j)),
            pl.BlockSpec((tm, tn), lambda i, j: (i, j)),
        ],
        out_specs=pl.BlockSpec((tm, tn), lambda i, j: (i, j)),
        out_shape=jax.ShapeDtypeStruct((m, n), jnp.float32),
        compiler_params=_params(("parallel", "parallel"), 40),
        name="matmul_residual",
    )(a, w, res)


def _attention_kernel(q_ref, k_ref, v_ref, o_ref, q_s, z_s, w_s, c_ref, acc_ref, *, tk):
    i = pl.program_id(2)
    tq, dh = q_ref.shape
    nsub = tq // tk
    q_s[...] = (q_ref[...] * (LOG2E / math.sqrt(dh))).astype(jnp.bfloat16)

    rows = lax.broadcasted_iota(jnp.int32, (tk, tk), 0)
    cols = lax.broadcasted_iota(jnp.int32, (tk, tk), 1)
    tri = (rows >= cols).astype(jnp.bfloat16)
    causal = cols < rows

    c_ref[...] = jnp.zeros_like(c_ref)
    acc_ref[...] = jnp.zeros_like(acc_ref)

    def scores(q, kb):
        return lax.dot_general(q, kb, (((1,), (1,)), ((), ())),
                               preferred_element_type=jnp.float32)

    def weights(z, c, mask):
        sp = jnp.maximum(z, 0.0) + jnp.log(1.0 + jnp.exp2(-jnp.abs(z))) * LOG2E
        if mask is not None:
            sp = jnp.where(mask, sp, 0.0)
        incl = jnp.dot(sp.astype(jnp.bfloat16), tri, preferred_element_type=jnp.float32)
        w = jnp.exp2(z - incl - c)
        if mask is not None:
            w = jnp.where(mask, w, 0.0)
        return w.astype(jnp.bfloat16), incl[:, 0:1]

    def key_start(t, u):
        blk = i * nsub - 1 - (t * UNROLL + u)
        return pl.multiple_of(jnp.maximum(blk, 0) * tk, tk)

    def stage_scores(t, par):
        for u in range(UNROLL):
            z_s[par, u] = scores(q_s[...], k_ref[pl.ds(key_start(t, u), tk), :])

    def stage_weights(par):
        for u in range(UNROLL):
            w, total = weights(z_s[par, u], c_ref[...], None)
            w_s[par, u] = w
            c_ref[...] += total

    def stage_values(t, par):
        for u in range(UNROLL):
            acc_ref[...] += jnp.dot(w_s[par, u], v_ref[pl.ds(key_start(t, u), tk), :],
                                    preferred_element_type=jnp.float32)

    stage_scores(0, 0)
    w_s[1] = jnp.zeros(w_s.shape[1:], w_s.dtype)

    def block(start, r0, nrows, mask):
        rsl = pl.ds(r0, nrows)
        w, total = weights(scores(q_s[rsl, :], k_ref[pl.ds(start, tk), :]),
                           c_ref[rsl, :], mask)
        acc_ref[rsl, :] += jnp.dot(w, v_ref[pl.ds(start, tk), :],
                                   preferred_element_type=jnp.float32)
        c_ref[rsl, :] += total

    for b in range(nsub - 1, -1, -1):
        start = pl.multiple_of(i * tq + b * tk, tk)
        block(start, b * tk, tk, causal)
        if b < nsub - 1:
            block(start, (b + 1) * tk, tq - (b + 1) * tk, None)

    def body(carry):
        m, _ = carry
        stage_scores(2 * m + 1, 1)
        stage_values(2 * m - 1, 1)
        stage_weights(0)
        c_min = jnp.min(c_ref[...])
        stage_scores(2 * m + 2, 0)
        stage_values(2 * m, 0)
        stage_weights(1)
        return m + 1, c_min

    def cond(carry):
        m, c_min = carry
        return jnp.logical_and(m < n_pairs, c_min < SKIP_BITS)

    n_pairs = i * nsub // (2 * UNROLL)
    m_done, _ = lax.while_loop(cond, body, (jnp.int32(0), jnp.min(c_ref[...])))
    stage_values(2 * m_done - 1, 1)
    o_ref[...] = acc_ref[...].astype(o_ref.dtype)


def _attention(proj_b, kv, *, tq=1024, tk=256):
    bsz, s, _ = proj_b.shape
    kern = functools.partial(_attention_kernel, tk=tk)
    return pl.pallas_call(
        kern,
        grid=(bsz, N_HEADS, s // tq),
        in_specs=[
            pl.BlockSpec((None, tq, HEAD_DIM), lambda b, h, i: (b, i, h)),
            pl.BlockSpec((None, s, HEAD_DIM), lambda b, h, i: (b, 0, h)),
            pl.BlockSpec((None, s, HEAD_DIM), lambda b, h, i: (b, 0, N_HEADS + h)),
        ],
        out_specs=pl.BlockSpec((None, tq, HEAD_DIM), lambda b, h, i: (b, i, h)),
        out_shape=jax.ShapeDtypeStruct((bsz, s, N_HEADS * HEAD_DIM), jnp.float32),
        scratch_shapes=[
            pltpu.VMEM((tq, HEAD_DIM), jnp.bfloat16),
            pltpu.VMEM((2, UNROLL, tq, tk), jnp.float32),
            pltpu.VMEM((2, UNROLL, tq, tk), jnp.bfloat16),
            pltpu.VMEM((tq, 1), jnp.float32),
            pltpu.VMEM((tq, HEAD_DIM), jnp.float32),
        ],
        compiler_params=_params(("parallel", "parallel", "arbitrary"), 40),
        name="attention",
    )(proj_b, kv, kv)


def _gated_out_norm_kernel(o_ref, g_ref, w_ref, x_ref, fn_ref, out_ref):
    a = (o_ref[...] * _silu(g_ref[...])).astype(jnp.bfloat16)
    y = x_ref[...] + jnp.dot(a, w_ref[...], preferred_element_type=jnp.float32)
    ms = jnp.mean(y * y, axis=-1, keepdims=True)
    out_ref[...] = y * lax.rsqrt(ms + EPS) * fn_ref[...]


def _gated_out_norm(o, proj_b, w, x1, fn, *, tm=256):
    m, d = x1.shape
    return pl.pallas_call(
        _gated_out_norm_kernel,
        grid=(m // tm,),
        in_specs=[
            pl.BlockSpec((tm, d), lambda i: (i, 0)),
            pl.BlockSpec((tm, d), lambda i: (i, 1)),
            pl.BlockSpec((d, d), lambda i: (0, 0)),
            pl.BlockSpec((tm, d), lambda i: (i, 0)),
            pl.BlockSpec((1, d), lambda i: (0, 0)),
        ],
        out_specs=pl.BlockSpec((tm, d), lambda i: (i, 0)),
        out_shape=jax.ShapeDtypeStruct((m, d), jnp.float32),
        compiler_params=_params(("parallel",), 48),
        name="gated_out_norm",
    )(o, proj_b, w, x1, fn)


def kernel(x, a_norm, a_w_in, a_conv_w, a_conv_b, a_w_r, a_b_r, a_w_i, a_b_i, a_lambda,
           a_w_out, kv_norm, w_kv, b_norm, b_w_in, b_w_out, final_norm):
    bsz, s, d = x.shape
    m = bsz * s
    bf = jnp.bfloat16
    x2 = x.reshape(m, d)

    proj_a = _norm_matmul(x2, a_norm[0:1], a_w_in[0].astype(bf), jnp.float32,
                          name="norm_matmul_a")
    c = proj_a.shape[1] // 2
    yg = _lru(proj_a.reshape(bsz, s, 2 * c), a_conv_w[0], a_conv_b[0:1],
              a_w_r[0].astype(bf), a_b_r[0:1], a_w_i[0].astype(bf), a_b_i[0:1],
              a_lambda[0:1])
    x1 = _matmul_residual(yg.reshape(m, c), a_w_out[0].astype(bf), x2)

    kv = _norm_matmul(x1, kv_norm.reshape(1, d), w_kv.astype(bf), bf, name="norm_matmul_kv")
    proj_b = _norm_matmul(x1, b_norm[0:1], b_w_in[0].astype(bf), jnp.float32,
                          name="norm_matmul_b")
    da = proj_b.shape[1] // 2
    o = _attention(proj_b.reshape(bsz, s, 2 * da), kv.reshape(bsz, s, 2 * da))
    out = _gated_out_norm(o.reshape(m, da), proj_b, b_w_out[0].astype(bf), x1,
                          final_norm.reshape(1, d))
    return out.reshape(bsz, s, d)
```

```python
import functools
import math

import jax
import jax.numpy as jnp
from jax import lax
from jax.experimental import pallas as pl
from jax.experimental.pallas import tpu as pltpu

EPS = 1e-6
LOG2E = 1.0 / math.log(2.0)
LRU_C = 8.0
CONV_W = 4
LRU_BW = 256
N_HEADS = 16
HEAD_DIM = 128
SKIP_BITS = 200.0
DONE_BITS = 1.0e6

SUBLANES = 8

MIB = 1024 * 1024


def _params(semantics, vmem_mib):
    return pltpu.CompilerParams(dimension_semantics=semantics,
                                vmem_limit_bytes=vmem_mib * MIB)


def _silu(g):
    return g * jax.nn.sigmoid(g)


def _norm_matmul_kernel(x_ref, g_ref, w_ref, o_ref, h_ref):
    @pl.when(pl.program_id(1) == 0)
    def _():
        x = x_ref[...]
        ms = jnp.mean(x * x, axis=-1, keepdims=True)
        h_ref[...] = (x * lax.rsqrt(ms + EPS) * g_ref[...]).astype(h_ref.dtype)

    o_ref[...] = jnp.dot(h_ref[...], w_ref[...],
                         preferred_element_type=jnp.float32).astype(o_ref.dtype)


def _norm_matmul(x, g, w, out_dtype, *, tm=1024, tn=512, name):
    m, k = x.shape
    n = w.shape[1]
    return pl.pallas_call(
        _norm_matmul_kernel,
        grid=(m // tm, n // tn),
        in_specs=[
            pl.BlockSpec((tm, k), lambda i, j: (i, 0)),
            pl.BlockSpec((1, k), lambda i, j: (0, 0)),
            pl.BlockSpec((k, tn), lambda i, j: (0, j)),
        ],
        out_specs=pl.BlockSpec((tm, tn), lambda i, j: (i, j)),
        out_shape=jax.ShapeDtypeStruct((m, n), out_dtype),
        scratch_shapes=[pltpu.VMEM((tm, k), jnp.bfloat16)],
        compiler_params=_params(("parallel", "arbitrary"), 40),
        name=name,
    )(x, g, w)


def _lru_kernel(xb_ref, gate_ref, cw_ref, cb_ref, wr_ref, br_ref, wi_ref, bi_ref,
                lam_ref, o_ref, xext, a_s, u_s, h_s):
    t = pl.program_id(2)
    tt, cw = xb_ref.shape

    @pl.when(t == 0)
    def _():
        xext[0:SUBLANES, :] = jnp.zeros((SUBLANES, cw), jnp.float32)
        h_s[...] = jnp.zeros_like(h_s)

    @pl.when(t > 0)
    def _():
        xext[0:SUBLANES, :] = xext[tt:tt + SUBLANES, :]

    xext[SUBLANES:SUBLANES + tt, :] = xb_ref[...]

    xc = cb_ref[...] + cw_ref[0:1, :] * xext[SUBLANES - 3:SUBLANES - 3 + tt, :]
    for w in range(1, CONV_W):
        off = SUBLANES - (CONV_W - 1) + w
        xc = xc + cw_ref[w:w + 1, :] * xext[off:off + tt, :]

    xcb = xc.astype(jnp.bfloat16)
    for n in range(cw // LRU_BW):
        sl = slice(n * LRU_BW, (n + 1) * LRU_BW)
        xn = xcb[:, sl]
        r = jax.nn.sigmoid(
            jnp.dot(xn, wr_ref[n], preferred_element_type=jnp.float32) + br_ref[:, sl])
        i = jax.nn.sigmoid(
            jnp.dot(xn, wi_ref[n], preferred_element_type=jnp.float32) + bi_ref[:, sl])
        neg_lam = -lam_ref[:, sl]
        softplus = jnp.maximum(neg_lam, 0.0) + jnp.log1p(jnp.exp(-jnp.abs(neg_lam)))
        log_a = -LRU_C * r * softplus
        a = jnp.exp(log_a)
        one_minus_a2 = -jnp.tanh(log_a) * (a * a + 1.0)
        mult = jnp.sqrt(jnp.maximum(one_minus_a2, 0.0))
        a_s[:, sl] = a
        u_s[:, sl] = mult * (i * xc[:, sl])

    def step(s, h):
        h = a_s[pl.ds(s, 1), :] * h + u_s[pl.ds(s, 1), :]
        u_s[pl.ds(s, 1), :] = h
        return h

    h_s[...] = lax.fori_loop(0, tt, step, h_s[...], unroll=8)
    o_ref[...] = (u_s[...] * _silu(gate_ref[...])).astype(o_ref.dtype)


def _lru(proj, conv_w, conv_b, w_r, b_r, w_i, b_i, lam, *, tt=256, cw=1280):
    bsz, s, two_c = proj.shape
    c = two_c // 2
    nc = c // cw
    nb = cw // LRU_BW
    row = lambda b, ci, t: (0, ci)
    return pl.pallas_call(
        _lru_kernel,
        grid=(bsz, nc, s // tt),
        in_specs=[
            pl.BlockSpec((None, tt, cw), lambda b, ci, t: (b, t, ci)),
            pl.BlockSpec((None, tt, cw), lambda b, ci, t: (b, t, nc + ci)),
            pl.BlockSpec((CONV_W, cw), row),
            pl.BlockSpec((1, cw), row),
            pl.BlockSpec((nb, LRU_BW, LRU_BW), lambda b, ci, t: (ci, 0, 0)),
            pl.BlockSpec((1, cw), row),
            pl.BlockSpec((nb, LRU_BW, LRU_BW), lambda b, ci, t: (ci, 0, 0)),
            pl.BlockSpec((1, cw), row),
            pl.BlockSpec((1, cw), row),
        ],
        out_specs=pl.BlockSpec((None, tt, cw), lambda b, ci, t: (b, t, ci)),
        out_shape=jax.ShapeDtypeStruct((bsz, s, c), jnp.bfloat16),
        scratch_shapes=[
            pltpu.VMEM((tt + SUBLANES, cw), jnp.float32),
            pltpu.VMEM((tt, cw), jnp.float32),
            pltpu.VMEM((tt, cw), jnp.float32),
            pltpu.VMEM((1, cw), jnp.float32),
        ],
        compiler_params=_params(("parallel", "parallel", "arbitrary"), 40),
        name="lru",
    )(proj, proj, conv_w, conv_b, w_r, b_r, w_i, b_i, lam)


def _matmul_residual_kernel(a_ref, w_ref, r_ref, o_ref):
    o_ref[...] = r_ref[...] + jnp.dot(a_ref[...], w_ref[...],
                                      preferred_element_type=jnp.float32)


def _matmul_residual(a, w, res, *, tm=1024, tn=512):
    m, k = a.shape
    n = w.shape[1]
    return pl.pallas_call(
        _matmul_residual_kernel,
        grid=(m // tm, n // tn),
        in_specs=[
            pl.BlockSpec((tm, k), lambda i, j: (i, 0)),
            pl.BlockSpec((k, tn), lambda i, j: (0, j)),
            pl.BlockSpec((tm, tn), lambda i, j: (i, j)),
        ],
        out_specs=pl.BlockSpec((tm, tn), lambda i, j: (i, j)),
        out_shape=jax.ShapeDtypeStruct((m, n), jnp.float32),
        compiler_params=_params(("parallel", "parallel"), 40),
        name="matmul_residual",
    )(a, w, res)


def _attention_kernel(q_ref, k_ref, v_ref, o_ref, q_s, z_s, w_s, c_ref, acc_ref, *, tk):
    i = pl.program_id(2)
    tq, dh = q_ref.shape
    nsub = tq // tk
    q_s[...] = (q_ref[...] * (LOG2E / math.sqrt(dh))).astype(jnp.bfloat16)

    rows = lax.broadcasted_iota(jnp.int32, (tk, tk), 0)
    cols = lax.broadcasted_iota(jnp.int32, (tk, tk), 1)
    tri = (rows >= cols).astype(jnp.bfloat16)
    causal = cols < rows

    c_ref[...] = jnp.zeros_like(c_ref)
    acc_ref[...] = jnp.zeros_like(acc_ref)

    def sub(a):
        return pl.ds(a * tk, tk)

    def key_block(d, a):
        blk = i * nsub + a - d
        return pl.ds(pl.multiple_of(jnp.maximum(blk, 0) * tk, tk), tk), blk >= 0

    def scores(q, kb):
        return lax.dot_general(q, kb, (((1,), (1,)), ((), ())),
                               preferred_element_type=jnp.float32)

    def weights(z, c, mask):
        sp = jnp.maximum(z, 0.0) + jnp.log(1.0 + jnp.exp2(-jnp.abs(z))) * LOG2E
        if mask is not None:
            sp = jnp.where(mask, sp, 0.0)
        incl = jnp.dot(sp.astype(jnp.bfloat16), tri, preferred_element_type=jnp.float32)
        w = jnp.exp2(z - incl - c)
        if mask is not None:
            w = jnp.where(mask, w, 0.0)
        return w.astype(jnp.bfloat16), incl[:, 0:1]

    def stage_scores(d, par):
        for a in range(nsub):
            keys, _ = key_block(d, a)
            z_s[par, sub(a), :] = scores(q_s[sub(a), :], k_ref[keys, :])

    def stage_weights(d, par):
        for a in range(nsub):
            _, exists = key_block(d, a)
            c_ref[sub(a), :] += jnp.where(exists, 0.0, DONE_BITS)
        w, total = weights(z_s[par], c_ref[...], None)
        w_s[par] = w
        c_ref[...] += total

    def stage_values(d, par):
        for a in range(nsub):
            keys, _ = key_block(d, a)
            acc_ref[sub(a), :] += jnp.dot(w_s[par, sub(a), :], v_ref[keys, :],
                                          preferred_element_type=jnp.float32)

    stage_scores(1, 0)

    for a in range(nsub):
        keys, _ = key_block(0, a)
        w, total = weights(scores(q_s[sub(a), :], k_ref[keys, :]), c_ref[sub(a), :], causal)
        acc_ref[sub(a), :] += jnp.dot(w, v_ref[keys, :], preferred_element_type=jnp.float32)
        c_ref[sub(a), :] += total

    stage_scores(2, 1)
    stage_weights(1, 0)

    def body(carry):
        m, _ = carry
        d = 2 * m + 2
        stage_scores(d + 1, 0)
        stage_values(d - 1, 0)
        stage_weights(d, 1)
        c_min = jnp.min(c_ref[...])
        stage_scores(d + 2, 1)
        stage_values(d, 1)
        stage_weights(d + 1, 0)
        return m + 1, c_min

    def cond(carry):
        m, c_min = carry
        return jnp.logical_and(m < n_pairs, c_min < SKIP_BITS)

    n_pairs = (i * nsub + nsub - 2) // 2
    m_done, _ = lax.while_loop(cond, body, (jnp.int32(0), jnp.min(c_ref[...])))
    stage_values(2 * m_done + 1, 0)
    o_ref[...] = acc_ref[...].astype(o_ref.dtype)


def _attention(proj_b, kv, *, tq=1024, tk=256):
    bsz, s, _ = proj_b.shape
    kern = functools.partial(_attention_kernel, tk=tk)
    return pl.pallas_call(
        kern,
        grid=(bsz, N_HEADS, s // tq),
        in_specs=[
            pl.BlockSpec((None, tq, HEAD_DIM), lambda b, h, i: (b, i, h)),
            pl.BlockSpec((None, s, HEAD_DIM), lambda b, h, i: (b, 0, h)),
            pl.BlockSpec((None, s, HEAD_DIM), lambda b, h, i: (b, 0, N_HEADS + h)),
        ],
        out_specs=pl.BlockSpec((None, tq, HEAD_DIM), lambda b, h, i: (b, i, h)),
        out_shape=jax.ShapeDtypeStruct((bsz, s, N_HEADS * HEAD_DIM), jnp.float32),
        scratch_shapes=[
            pltpu.VMEM((tq, HEAD_DIM), jnp.bfloat16),
            pltpu.VMEM((2, tq, tk), jnp.float32),
            pltpu.VMEM((2, tq, tk), jnp.bfloat16),
            pltpu.VMEM((tq, 1), jnp.float32),
            pltpu.VMEM((tq, HEAD_DIM), jnp.float32),
        ],
        compiler_params=_params(("parallel", "parallel", "arbitrary"), 40),
        name="attention",
    )(proj_b, kv, kv)


def _gated_out_norm_kernel(o_ref, g_ref, w_ref, x_ref, fn_ref, out_ref):
    a = (o_ref[...] * _silu(g_ref[...])).astype(jnp.bfloat16)
    y = x_ref[...] + jnp.dot(a, w_ref[...], preferred_element_type=jnp.float32)
    ms = jnp.mean(y * y, axis=-1, keepdims=True)
    out_ref[...] = y * lax.rsqrt(ms + EPS) * fn_ref[...]


def _gated_out_norm(o, proj_b, w, x1, fn, *, tm=256):
    m, d = x1.shape
    return pl.pallas_call(
        _gated_out_norm_kernel,
        grid=(m // tm,),
        in_specs=[
            pl.BlockSpec((tm, d), lambda i: (i, 0)),
            pl.BlockSpec((tm, d), lambda i: (i, 1)),
            pl.BlockSpec((d, d), lambda i: (0, 0)),
            pl.BlockSpec((tm, d), lambda i: (i, 0)),
            pl.BlockSpec((1, d), lambda i: (0, 0)),
        ],
        out_specs=pl.BlockSpec((tm, d), lambda i: (i, 0)),
        out_shape=jax.ShapeDtypeStruct((m, d), jnp.float32),
        compiler_params=_params(("parallel",), 48),
        name="gated_out_norm",
    )(o, proj_b, w, x1, fn)


def kernel(x, a_norm, a_w_in, a_conv_w, a_conv_b, a_w_r, a_b_r, a_w_i, a_b_i, a_lambda,
           a_w_out, kv_norm, w_kv, b_norm, b_w_in, b_w_out, final_norm):
    bsz, s, d = x.shape
    m = bsz * s
    bf = jnp.bfloat16
    x2 = x.reshape(m, d)

    proj_a = _norm_matmul(x2, a_norm[0:1], a_w_in[0].astype(bf), jnp.float32,
                          name="norm_matmul_a")
    c = proj_a.shape[1] // 2
    yg = _lru(proj_a.reshape(bsz, s, 2 * c), a_conv_w[0], a_conv_b[0:1],
              a_w_r[0].astype(bf), a_b_r[0:1], a_w_i[0].astype(bf), a_b_i[0:1],
              a_lambda[0:1])
    x1 = _matmul_residual(yg.reshape(m, c), a_w_out[0].astype(bf), x2)

    kv = _norm_matmul(x1, kv_norm.reshape(1, d), w_kv.astype(bf), bf, name="norm_matmul_kv")
    proj_b = _norm_matmul(x1, b_norm[0:1], b_w_in[0].astype(bf), jnp.float32,
                          name="norm_matmul_b")
    da = proj_b.shape[1] // 2
    o = _attention(proj_b.reshape(bsz, s, 2 * da), kv.reshape(bsz, s, 2 * da))
    out = _gated_out_norm(o.reshape(m, da), proj_b, b_w_out[0].astype(bf), x1,
                          final_norm.reshape(1, d))
    return out.reshape(bsz, s, d)
```

```python
import functools
import math

import jax
import jax.numpy as jnp
from jax import lax
from jax.experimental import pallas as pl
from jax.experimental.pallas import tpu as pltpu

EPS = 1e-6
LOG2E = 1.0 / math.log(2.0)
LRU_C = 8.0
CONV_W = 4
LRU_BW = 256
N_HEADS = 16
HEAD_DIM = 128
SKIP_BITS = 200.0
DONE_BITS = 1.0e6

SUBLANES = 8
LANES = 128
SEG = SUBLANES
PAD = 4

MIB = 1024 * 1024


def _params(semantics, vmem_mib):
    return pltpu.CompilerParams(dimension_semantics=semantics,
                                vmem_limit_bytes=vmem_mib * MIB)


def _silu(g):
    return g * jax.nn.sigmoid(g)


def _norm_matmul_kernel(x_ref, g_ref, w_ref, o_ref, h_ref):
    @pl.when(pl.program_id(1) == 0)
    def _():
        x = x_ref[...]
        ms = jnp.mean(x * x, axis=-1, keepdims=True)
        h_ref[...] = (x * lax.rsqrt(ms + EPS) * g_ref[...]).astype(h_ref.dtype)

    o_ref[...] = jnp.dot(h_ref[...], w_ref[...],
                         preferred_element_type=jnp.float32).astype(o_ref.dtype)


def _norm_matmul(x, g, w, out_dtype, *, tm=1024, tn=512, name):
    m, k = x.shape
    n = w.shape[1]
    return pl.pallas_call(
        _norm_matmul_kernel,
        grid=(m // tm, n // tn),
        in_specs=[
            pl.BlockSpec((tm, k), lambda i, j: (i, 0)),
            pl.BlockSpec((1, k), lambda i, j: (0, 0)),
            pl.BlockSpec((k, tn), lambda i, j: (0, j)),
        ],
        out_specs=pl.BlockSpec((tm, tn), lambda i, j: (i, j)),
        out_shape=jax.ShapeDtypeStruct((m, n), out_dtype),
        scratch_shapes=[pltpu.VMEM((tm, k), jnp.bfloat16)],
        compiler_params=_params(("parallel", "arbitrary"), 40),
        name=name,
    )(x, g, w)


def _lru_kernel(xb_ref, gate_ref, cw_ref, cb_ref, wr_ref, br_ref, wi_ref, bi_ref,
                lam_ref, o_ref, xext, a_s, u_s, hl_s, al_s, y_s, carry_s):
    t = pl.program_id(2)
    tt, cw = xb_ref.shape
    ns = cw // LANES
    seg = tt // SEG
    pitch = seg + PAD

    @pl.when(t == 0)
    def _():
        xext[:, 0:SUBLANES, :] = jnp.zeros((ns, SUBLANES, LANES), jnp.float32)
        carry_s[...] = jnp.zeros_like(carry_s)

    @pl.when(t > 0)
    def _():
        xext[:, 0:SUBLANES, :] = xext[:, tt:tt + SUBLANES, :]

    def lanes(s):
        return slice(s * LANES, (s + 1) * LANES)

    for s in range(ns):
        xext[s, SUBLANES:SUBLANES + tt, :] = xb_ref[:, lanes(s)]

    for n in range(cw // LRU_BW):
        slabs = range(n * LRU_BW // LANES, (n + 1) * LRU_BW // LANES)
        parts = []
        for s in slabs:
            xc = cb_ref[:, lanes(s)]
            for w in range(CONV_W):
                off = SUBLANES - (CONV_W - 1) + w
                xc = xc + cw_ref[w:w + 1, lanes(s)] * xext[s, off:off + tt, :]
            parts.append(xc)
        xc = jnp.concatenate(parts, axis=1)
        sl = slice(n * LRU_BW, (n + 1) * LRU_BW)
        xn = xc.astype(jnp.bfloat16)
        r = jax.nn.sigmoid(
            jnp.dot(xn, wr_ref[n], preferred_element_type=jnp.float32) + br_ref[:, sl])
        i = jax.nn.sigmoid(
            jnp.dot(xn, wi_ref[n], preferred_element_type=jnp.float32) + bi_ref[:, sl])
        neg_lam = -lam_ref[:, sl]
        softplus = jnp.maximum(neg_lam, 0.0) + jnp.log1p(jnp.exp(-jnp.abs(neg_lam)))
        log_a = r * (-LRU_C * softplus)
        a = jnp.exp(log_a)
        one_minus_a2 = -jnp.tanh(log_a) * (a * a + 1.0)
        mult = jnp.where(one_minus_a2 > 0.0, one_minus_a2 * lax.rsqrt(one_minus_a2), 0.0)
        u = mult * (i * xc)
        for j, s in enumerate(slabs):
            for g in range(SEG):
                a_s[s, g * pitch:g * pitch + seg, :] = a[g * seg:(g + 1) * seg, lanes(j)]
                u_s[s, g * pitch:g * pitch + seg, :] = u[g * seg:(g + 1) * seg, lanes(j)]

    for s in range(ns):
        h = jnp.zeros((SEG, LANES), jnp.float32)
        prod = jnp.ones((SEG, LANES), jnp.float32)
        for p in range(seg):
            ap = a_s[s, pl.ds(p, SEG, stride=pitch), :]
            h = ap * h + u_s[s, pl.ds(p, SEG, stride=pitch), :]
            prod = ap * prod
            hl_s[s, p] = h
            al_s[s, p] = prod
        entering = [carry_s[s]]
        for g in range(SEG - 1):
            entering.append(prod[g:g + 1] * entering[-1] + h[g:g + 1])
        carry_s[s] = prod[SEG - 1:SEG] * entering[-1] + h[SEG - 1:SEG]
        h_in = jnp.concatenate(entering, axis=0)
        for p in range(seg):
            y_s[s, pl.ds(p, SEG, stride=pitch), :] = hl_s[s, p] + al_s[s, p] * h_in

    for s in range(ns):
        for g in range(SEG):
            rows = slice(g * seg, (g + 1) * seg)
            y = y_s[s, g * pitch:g * pitch + seg, :]
            o_ref[rows, lanes(s)] = (y * _silu(gate_ref[rows, lanes(s)])).astype(o_ref.dtype)


def _lru(proj, conv_w, conv_b, w_r, b_r, w_i, b_i, lam, *, tt=256, cw=1280):
    bsz, s, two_c = proj.shape
    c = two_c // 2
    nc = c // cw
    nb = cw // LRU_BW
    ns = cw // LANES
    seg = tt // SEG
    row = lambda b, ci, t: (0, ci)
    return pl.pallas_call(
        _lru_kernel,
        grid=(bsz, nc, s // tt),
        in_specs=[
            pl.BlockSpec((None, tt, cw), lambda b, ci, t: (b, t, ci)),
            pl.BlockSpec((None, tt, cw), lambda b, ci, t: (b, t, nc + ci)),
            pl.BlockSpec((CONV_W, cw), row),
            pl.BlockSpec((1, cw), row),
            pl.BlockSpec((nb, LRU_BW, LRU_BW), lambda b, ci, t: (ci, 0, 0)),
            pl.BlockSpec((1, cw), row),
            pl.BlockSpec((nb, LRU_BW, LRU_BW), lambda b, ci, t: (ci, 0, 0)),
            pl.BlockSpec((1, cw), row),
            pl.BlockSpec((1, cw), row),
        ],
        out_specs=pl.BlockSpec((None, tt, cw), lambda b, ci, t: (b, t, ci)),
        out_shape=jax.ShapeDtypeStruct((bsz, s, c), jnp.bfloat16),
        scratch_shapes=[
            pltpu.VMEM((ns, tt + SUBLANES, LANES), jnp.float32),
            pltpu.VMEM((ns, SEG * (seg + PAD), LANES), jnp.float32),
            pltpu.VMEM((ns, SEG * (seg + PAD), LANES), jnp.float32),
            pltpu.VMEM((ns, seg, SEG, LANES), jnp.float32),
            pltpu.VMEM((ns, seg, SEG, LANES), jnp.float32),
            pltpu.VMEM((ns, SEG * (seg + PAD), LANES), jnp.float32),
            pltpu.VMEM((ns, 1, LANES), jnp.float32),
        ],
        compiler_params=_params(("parallel", "parallel", "arbitrary"), 40),
        name="lru",
    )(proj, proj, conv_w, conv_b, w_r, b_r, w_i, b_i, lam)


def _matmul_residual_kernel(a_ref, w_ref, r_ref, o_ref):
    o_ref[...] = r_ref[...] + jnp.dot(a_ref[...], w_ref[...],
                                      preferred_element_type=jnp.float32)


def _matmul_residual(a, w, res, *, tm=1024, tn=512):
    m, k = a.shape
    n = w.shape[1]
    return pl.pallas_call(
        _matmul_residual_kernel,
        grid=(m // tm, n // tn),
        in_specs=[
            pl.BlockSpec((tm, k), lambda i, j: (i, 0)),
            pl.BlockSpec((k, tn), lambda i, j: (0, j)),
            pl.BlockSpec((tm, tn), lambda i, j: (i, j)),
        ],
        out_specs=pl.BlockSpec((tm, tn), lambda i, j: (i, j)),
        out_shape=jax.ShapeDtypeStruct((m, n), jnp.float32),
        compiler_params=_params(("parallel", "parallel"), 40),
        name="matmul_residual",
    )(a, w, res)


def _attention_kernel(q_ref, k_ref, v_ref, o_ref, q_s, z_s, w_s, c_ref, acc_ref, *, tk):
    i = pl.program_id(2)
    tq, dh = q_ref.shape
    nsub = tq // tk
    q_s[...] = (q_ref[...] * (LOG2E / math.sqrt(dh))).astype(jnp.bfloat16)

    rows = lax.broadcasted_iota(jnp.int32, (tk, tk), 0)
    cols = lax.broadcasted_iota(jnp.int32, (tk, tk), 1)
    tri = (rows >= cols).astype(jnp.bfloat16)
    causal = cols < rows

    c_ref[...] = jnp.zeros_like(c_ref)
    acc_ref[...] = jnp.zeros_like(acc_ref)

    def sub(a):
        return pl.ds(a * tk, tk)

    def key_block(d, a):
        blk = i * nsub + a - d
        return pl.ds(pl.multiple_of(jnp.maximum(blk, 0) * tk, tk), tk), blk >= 0

    def scores(q, kb):
        return lax.dot_general(q, kb, (((1,), (1,)), ((), ())),
                               preferred_element_type=jnp.float32)

    def weights(z, c, mask):
        sp = jnp.maximum(z, 0.0) + jnp.log(1.0 + jnp.exp2(-jnp.abs(z))) * LOG2E
        if mask is not None:
            sp = jnp.where(mask, sp, 0.0)
        incl = jnp.dot(sp.astype(jnp.bfloat16), tri, preferred_element_type=jnp.float32)
        w = jnp.exp2(z - incl - c)
        if mask is not None:
            w = jnp.where(mask, w, 0.0)
        return w.astype(jnp.bfloat16), incl[:, 0:1]

    def stage_scores(d, par):
        for a in range(nsub):
            keys, _ = key_block(d, a)
            z_s[par, sub(a), :] = scores(q_s[sub(a), :], k_ref[keys, :])

    def stage_weights(d, par):
        for a in range(nsub):
            _, exists = key_block(d, a)
            c_ref[sub(a), :] += jnp.where(exists, 0.0, DONE_BITS)
        w, total = weights(z_s[par], c_ref[...], None)
        w_s[par] = w
        c_ref[...] += total

    def stage_values(d, par):
        for a in range(nsub):
            keys, _ = key_block(d, a)
            acc_ref[sub(a), :] += jnp.dot(w_s[par, sub(a), :], v_ref[keys, :],
                                          preferred_element_type=jnp.float32)

    stage_scores(1, 0)

    for a in range(nsub):
        keys, _ = key_block(0, a)
        w, total = weights(scores(q_s[sub(a), :], k_ref[keys, :]), c_ref[sub(a), :], causal)
        acc_ref[sub(a), :] += jnp.dot(w, v_ref[keys, :], preferred_element_type=jnp.float32)
        c_ref[sub(a), :] += total

    stage_scores(2, 1)
    stage_weights(1, 0)

    def body(carry):
        m, _ = carry
        d = 2 * m + 2
        stage_scores(d + 1, 0)
        stage_values(d - 1, 0)
        stage_weights(d, 1)
        c_min = jnp.min(c_ref[...])
        stage_scores(d + 2, 1)
        stage_values(d, 1)
        stage_weights(d + 1, 0)
        return m + 1, c_min

    def cond(carry):
        m, c_min = carry
        return jnp.logical_and(m < n_pairs, c_min < SKIP_BITS)

    n_pairs = (i * nsub + nsub - 2) // 2
    m_done, _ = lax.while_loop(cond, body, (jnp.int32(0), jnp.min(c_ref[...])))
    stage_values(2 * m_done + 1, 0)
    o_ref[...] = acc_ref[...].astype(o_ref.dtype)


def _attention(proj_b, kv, *, tq=1024, tk=256):
    bsz, s, _ = proj_b.shape
    kern = functools.partial(_attention_kernel, tk=tk)
    return pl.pallas_call(
        kern,
        grid=(bsz, N_HEADS, s // tq),
        in_specs=[
            pl.BlockSpec((None, tq, HEAD_DIM), lambda b, h, i: (b, i, h)),
            pl.BlockSpec((None, s, HEAD_DIM), lambda b, h, i: (b, 0, h)),
            pl.BlockSpec((None, s, HEAD_DIM), lambda b, h, i: (b, 0, N_HEADS + h)),
        ],
        out_specs=pl.BlockSpec((None, tq, HEAD_DIM), lambda b, h, i: (b, i, h)),
        out_shape=jax.ShapeDtypeStruct((bsz, s, N_HEADS * HEAD_DIM), jnp.float32),
        scratch_shapes=[
            pltpu.VMEM((tq, HEAD_DIM), jnp.bfloat16),
            pltpu.VMEM((2, tq, tk), jnp.float32),
            pltpu.VMEM((2, tq, tk), jnp.bfloat16),
            pltpu.VMEM((tq, 1), jnp.float32),
            pltpu.VMEM((tq, HEAD_DIM), jnp.float32),
        ],
        compiler_params=_params(("parallel", "parallel", "arbitrary"), 40),
        name="attention",
    )(proj_b, kv, kv)


def _gated_out_norm_kernel(o_ref, g_ref, w_ref, x_ref, fn_ref, out_ref):
    a = (o_ref[...] * _silu(g_ref[...])).astype(jnp.bfloat16)
    y = x_ref[...] + jnp.dot(a, w_ref[...], preferred_element_type=jnp.float32)
    ms = jnp.mean(y * y, axis=-1, keepdims=True)
    out_ref[...] = y * lax.rsqrt(ms + EPS) * fn_ref[...]


def _gated_out_norm(o, proj_b, w, x1, fn, *, tm=256):
    m, d = x1.shape
    return pl.pallas_call(
        _gated_out_norm_kernel,
        grid=(m // tm,),
        in_specs=[
            pl.BlockSpec((tm, d), lambda i: (i, 0)),
            pl.BlockSpec((tm, d), lambda i: (i, 1)),
            pl.BlockSpec((d, d), lambda i: (0, 0)),
            pl.BlockSpec((tm, d), lambda i: (i, 0)),
            pl.BlockSpec((1, d), lambda i: (0, 0)),
        ],
        out_specs=pl.BlockSpec((tm, d), lambda i: (i, 0)),
        out_shape=jax.ShapeDtypeStruct((m, d), jnp.float32),
        compiler_params=_params(("parallel",), 48),
        name="gated_out_norm",
    )(o, proj_b, w, x1, fn)


def kernel(x, a_norm, a_w_in, a_conv_w, a_conv_b, a_w_r, a_b_r, a_w_i, a_b_i, a_lambda,
           a_w_out, kv_norm, w_kv, b_norm, b_w_in, b_w_out, final_norm):
    bsz, s, d = x.shape
    m = bsz * s
    bf = jnp.bfloat16
    x2 = x.reshape(m, d)

    proj_a = _norm_matmul(x2, a_norm[0:1], a_w_in[0].astype(bf), jnp.float32,
                          name="norm_matmul_a")
    c = proj_a.shape[1] // 2
    yg = _lru(proj_a.reshape(bsz, s, 2 * c), a_conv_w[0], a_conv_b[0:1],
              a_w_r[0].astype(bf), a_b_r[0:1], a_w_i[0].astype(bf), a_b_i[0:1],
              a_lambda[0:1])
    x1 = _matmul_residual(yg.reshape(m, c), a_w_out[0].astype(bf), x2)

    kv = _norm_matmul(x1, kv_norm.reshape(1, d), w_kv.astype(bf), bf, name="norm_matmul_kv")
    proj_b = _norm_matmul(x1, b_norm[0:1], b_w_in[0].astype(bf), jnp.float32,
                          name="norm_matmul_b")
    da = proj_b.shape[1] // 2
    o = _attention(proj_b.reshape(bsz, s, 2 * da), kv.reshape(bsz, s, 2 * da))
    out = _gated_out_norm(o.reshape(m, da), proj_b, b_w_out[0].astype(bf), x1,
                          final_norm.reshape(1, d))
    return out.reshape(bsz, s, d)
```

```python
import functools
import math

import jax
import jax.numpy as jnp
from jax import lax
from jax.experimental import pallas as pl
from jax.experimental.pallas import tpu as pltpu

EPS = 1e-6
LOG2E = 1.0 / math.log(2.0)
LRU_C = 8.0
CONV_W = 4
LRU_BW = 256
N_HEADS = 16
HEAD_DIM = 128
SKIP_BITS = 200.0
DONE_BITS = 1.0e6

SUBLANES = 8
LANES = 128
SEG = SUBLANES
PAD = 4

MIB = 1024 * 1024


def _params(semantics, vmem_mib):
    return pltpu.CompilerParams(dimension_semantics=semantics,
                                vmem_limit_bytes=vmem_mib * MIB)


def _silu(g):
    return g * jax.nn.sigmoid(g)


def _norm_matmul_kernel(x_ref, g_ref, w_ref, o_ref, h_ref):
    @pl.when(pl.program_id(1) == 0)
    def _():
        x = x_ref[...]
        ms = jnp.mean(x * x, axis=-1, keepdims=True)
        h_ref[...] = (x * lax.rsqrt(ms + EPS) * g_ref[...]).astype(h_ref.dtype)

    o_ref[...] = jnp.dot(h_ref[...], w_ref[...],
                         preferred_element_type=jnp.float32).astype(o_ref.dtype)


def _norm_matmul(x, g, w, out_dtype, *, tm=1024, tn=1024, name):
    m, k = x.shape
    n = w.shape[1]
    return pl.pallas_call(
        _norm_matmul_kernel,
        grid=(m // tm, n // tn),
        in_specs=[
            pl.BlockSpec((tm, k), lambda i, j: (i, 0)),
            pl.BlockSpec((1, k), lambda i, j: (0, 0)),
            pl.BlockSpec((k, tn), lambda i, j: (0, j)),
        ],
        out_specs=pl.BlockSpec((tm, tn), lambda i, j: (i, j)),
        out_shape=jax.ShapeDtypeStruct((m, n), out_dtype),
        scratch_shapes=[pltpu.VMEM((tm, k), jnp.bfloat16)],
        compiler_params=_params(("parallel", "arbitrary"), 40),
        name=name,
    )(x, g, w)


def _lru_kernel(xb_ref, gate_ref, cw_ref, cb_ref, wr_ref, br_ref, wi_ref, bi_ref,
                lam_ref, o_ref, xext, a_s, u_s, hl_s, al_s, y_s, carry_s):
    t = pl.program_id(2)
    tt, cw = xb_ref.shape
    ns = cw // LANES
    seg = tt // SEG
    pitch = seg + PAD

    @pl.when(t == 0)
    def _():
        xext[:, 0:SUBLANES, :] = jnp.zeros((ns, SUBLANES, LANES), jnp.float32)
        carry_s[...] = jnp.zeros_like(carry_s)

    @pl.when(t > 0)
    def _():
        xext[:, 0:SUBLANES, :] = xext[:, tt:tt + SUBLANES, :]

    def lanes(s):
        return slice(s * LANES, (s + 1) * LANES)

    for s in range(ns):
        xext[s, SUBLANES:SUBLANES + tt, :] = xb_ref[:, lanes(s)]

    for n in range(cw // LRU_BW):
        slabs = range(n * LRU_BW // LANES, (n + 1) * LRU_BW // LANES)
        parts = []
        for s in slabs:
            xc = cb_ref[:, lanes(s)]
            for w in range(CONV_W):
                off = SUBLANES - (CONV_W - 1) + w
                xc = xc + cw_ref[w:w + 1, lanes(s)] * xext[s, off:off + tt, :]
            parts.append(xc)
        xc = jnp.concatenate(parts, axis=1)
        sl = slice(n * LRU_BW, (n + 1) * LRU_BW)
        xn = xc.astype(jnp.bfloat16)
        r = jax.nn.sigmoid(
            jnp.dot(xn, wr_ref[n], preferred_element_type=jnp.float32) + br_ref[:, sl])
        i = jax.nn.sigmoid(
            jnp.dot(xn, wi_ref[n], preferred_element_type=jnp.float32) + bi_ref[:, sl])
        neg_lam = -lam_ref[:, sl]
        softplus = jnp.maximum(neg_lam, 0.0) + jnp.log1p(jnp.exp(-jnp.abs(neg_lam)))
        log_a = r * (-LRU_C * softplus)
        a = jnp.exp(log_a)
        one_minus_a2 = -jnp.tanh(log_a) * (a * a + 1.0)
        mult = jnp.where(one_minus_a2 > 0.0, one_minus_a2 * lax.rsqrt(one_minus_a2), 0.0)
        u = mult * (i * xc)
        for j, s in enumerate(slabs):
            for g in range(SEG):
                a_s[s, g * pitch:g * pitch + seg, :] = a[g * seg:(g + 1) * seg, lanes(j)]
                u_s[s, g * pitch:g * pitch + seg, :] = u[g * seg:(g + 1) * seg, lanes(j)]

    for s in range(ns):
        h = jnp.zeros((SEG, LANES), jnp.float32)
        prod = jnp.ones((SEG, LANES), jnp.float32)
        for p in range(seg):
            ap = a_s[s, pl.ds(p, SEG, stride=pitch), :]
            h = ap * h + u_s[s, pl.ds(p, SEG, stride=pitch), :]
            prod = ap * prod
            hl_s[s, p] = h
            al_s[s, p] = prod
        entering = [carry_s[s]]
        for g in range(SEG - 1):
            entering.append(prod[g:g + 1] * entering[-1] + h[g:g + 1])
        carry_s[s] = prod[SEG - 1:SEG] * entering[-1] + h[SEG - 1:SEG]
        h_in = jnp.concatenate(entering, axis=0)
        for p in range(seg):
            y_s[s, pl.ds(p, SEG, stride=pitch), :] = hl_s[s, p] + al_s[s, p] * h_in

    for s in range(ns):
        for g in range(SEG):
            rows = slice(g * seg, (g + 1) * seg)
            y = y_s[s, g * pitch:g * pitch + seg, :]
            o_ref[rows, lanes(s)] = (y * _silu(gate_ref[rows, lanes(s)])).astype(o_ref.dtype)


def _lru(proj, conv_w, conv_b, w_r, b_r, w_i, b_i, lam, *, tt=512, cw=1280):
    bsz, s, two_c = proj.shape
    c = two_c // 2
    nc = c // cw
    nb = cw // LRU_BW
    ns = cw // LANES
    seg = tt // SEG
    row = lambda b, ci, t: (0, ci)
    return pl.pallas_call(
        _lru_kernel,
        grid=(bsz, nc, s // tt),
        in_specs=[
            pl.BlockSpec((None, tt, cw), lambda b, ci, t: (b, t, ci)),
            pl.BlockSpec((None, tt, cw), lambda b, ci, t: (b, t, nc + ci)),
            pl.BlockSpec((CONV_W, cw), row),
            pl.BlockSpec((1, cw), row),
            pl.BlockSpec((nb, LRU_BW, LRU_BW), lambda b, ci, t: (ci, 0, 0)),
            pl.BlockSpec((1, cw), row),
            pl.BlockSpec((nb, LRU_BW, LRU_BW), lambda b, ci, t: (ci, 0, 0)),
            pl.BlockSpec((1, cw), row),
            pl.BlockSpec((1, cw), row),
        ],
        out_specs=pl.BlockSpec((None, tt, cw), lambda b, ci, t: (b, t, ci)),
        out_shape=jax.ShapeDtypeStruct((bsz, s, c), jnp.bfloat16),
        scratch_shapes=[
            pltpu.VMEM((ns, tt + SUBLANES, LANES), jnp.float32),
            pltpu.VMEM((ns, SEG * (seg + PAD), LANES), jnp.float32),
            pltpu.VMEM((ns, SEG * (seg + PAD), LANES), jnp.float32),
            pltpu.VMEM((ns, seg, SEG, LANES), jnp.float32),
            pltpu.VMEM((ns, seg, SEG, LANES), jnp.float32),
            pltpu.VMEM((ns, SEG * (seg + PAD), LANES), jnp.float32),
            pltpu.VMEM((ns, 1, LANES), jnp.float32),
        ],
        compiler_params=_params(("parallel", "parallel", "arbitrary"), 40),
        name="lru",
    )(proj, proj, conv_w, conv_b, w_r, b_r, w_i, b_i, lam)


def _matmul_residual_kernel(a_ref, w_ref, r_ref, o_ref):
    o_ref[...] = r_ref[...] + jnp.dot(a_ref[...], w_ref[...],
                                      preferred_element_type=jnp.float32)


def _matmul_residual(a, w, res, *, tm=1024, tn=1024):
    m, k = a.shape
    n = w.shape[1]
    return pl.pallas_call(
        _matmul_residual_kernel,
        grid=(m // tm, n // tn),
        in_specs=[
            pl.BlockSpec((tm, k), lambda i, j: (i, 0)),
            pl.BlockSpec((k, tn), lambda i, j: (0, j)),
            pl.BlockSpec((tm, tn), lambda i, j: (i, j)),
        ],
        out_specs=pl.BlockSpec((tm, tn), lambda i, j: (i, j)),
        out_shape=jax.ShapeDtypeStruct((m, n), jnp.float32),
        compiler_params=_params(("parallel", "parallel"), 40),
        name="matmul_residual",
    )(a, w, res)


def _attention_kernel(q_ref, k_ref, v_ref, o_ref, q_s, z_s, w_s, c_ref, acc_ref, *, tk):
    i = pl.program_id(2)
    tq, dh = q_ref.shape
    nsub = tq // tk
    q_s[...] = (q_ref[...].astype(jnp.float32) * (LOG2E / math.sqrt(dh))).astype(jnp.bfloat16)

    rows = lax.broadcasted_iota(jnp.int32, (tk, tk), 0)
    cols = lax.broadcasted_iota(jnp.int32, (tk, tk), 1)
    tri = (rows >= cols).astype(jnp.bfloat16)
    causal = cols < rows

    c_ref[...] = jnp.zeros_like(c_ref)
    acc_ref[...] = jnp.zeros_like(acc_ref)

    def sub(a):
        return pl.ds(a * tk, tk)

    def key_block(d, a):
        blk = i * nsub + a - d
        return pl.ds(pl.multiple_of(jnp.maximum(blk, 0) * tk, tk), tk), blk >= 0

    def scores(q, kb):
        return lax.dot_general(q, kb, (((1,), (1,)), ((), ())),
                               preferred_element_type=jnp.float32)

    def weights(z, c, mask):
        sp = jnp.maximum(z, 0.0) + jnp.log(1.0 + jnp.exp2(-jnp.abs(z))) * LOG2E
        if mask is not None:
            sp = jnp.where(mask, sp, 0.0)
        incl = jnp.dot(sp.astype(jnp.bfloat16), tri, preferred_element_type=jnp.float32)
        w = jnp.exp2(z - incl - c)
        if mask is not None:
            w = jnp.where(mask, w, 0.0)
        return w.astype(jnp.bfloat16), incl[:, 0:1]

    def stage_scores(d, par):
        for a in range(nsub):
            keys, _ = key_block(d, a)
            z_s[par, sub(a), :] = scores(q_s[sub(a), :], k_ref[keys, :])

    def stage_weights(d, par):
        for a in range(nsub):
            _, exists = key_block(d, a)
            c_ref[sub(a), :] += jnp.where(exists, 0.0, DONE_BITS)
        w, total = weights(z_s[par], c_ref[...], None)
        w_s[par] = w
        c_ref[...] += total

    def stage_values(d, par):
        for a in range(nsub):
            keys, _ = key_block(d, a)
            acc_ref[sub(a), :] += jnp.dot(w_s[par, sub(a), :], v_ref[keys, :],
                                          preferred_element_type=jnp.float32)

    stage_scores(1, 0)

    for a in range(nsub):
        keys, _ = key_block(0, a)
        w, total = weights(scores(q_s[sub(a), :], k_ref[keys, :]), c_ref[sub(a), :], causal)
        acc_ref[sub(a), :] += jnp.dot(w, v_ref[keys, :], preferred_element_type=jnp.float32)
        c_ref[sub(a), :] += total

    stage_scores(2, 1)
    stage_weights(1, 0)

    def body(carry):
        m, _ = carry
        d = 2 * m + 2
        stage_scores(d + 1, 0)
        stage_values(d - 1, 0)
        stage_weights(d, 1)
        c_min = jnp.min(c_ref[...])
        stage_scores(d + 2, 1)
        stage_values(d, 1)
        stage_weights(d + 1, 0)
        return m + 1, c_min

    def cond(carry):
        m, c_min = carry
        return jnp.logical_and(m < n_pairs, c_min < SKIP_BITS)

    n_pairs = (i * nsub + nsub - 2) // 2
    m_done, _ = lax.while_loop(cond, body, (jnp.int32(0), jnp.min(c_ref[...])))
    stage_values(2 * m_done + 1, 0)
    o_ref[...] = acc_ref[...].astype(o_ref.dtype)


def _attention(proj_b, kv, *, tq=2048, tk=256):
    bsz, s, _ = proj_b.shape
    kern = functools.partial(_attention_kernel, tk=tk)
    return pl.pallas_call(
        kern,
        grid=(bsz, N_HEADS, s // tq),
        in_specs=[
            pl.BlockSpec((None, tq, HEAD_DIM), lambda b, h, i: (b, i, h)),
            pl.BlockSpec((None, s, HEAD_DIM), lambda b, h, i: (b, 0, h)),
            pl.BlockSpec((None, s, HEAD_DIM), lambda b, h, i: (b, 0, N_HEADS + h)),
        ],
        out_specs=pl.BlockSpec((None, tq, HEAD_DIM), lambda b, h, i: (b, i, h)),
        out_shape=jax.ShapeDtypeStruct((bsz, s, N_HEADS * HEAD_DIM), jnp.bfloat16),
        scratch_shapes=[
            pltpu.VMEM((tq, HEAD_DIM), jnp.bfloat16),
            pltpu.VMEM((2, tq, tk), jnp.float32),
            pltpu.VMEM((2, tq, tk), jnp.bfloat16),
            pltpu.VMEM((tq, 1), jnp.float32),
            pltpu.VMEM((tq, HEAD_DIM), jnp.float32),
        ],
        compiler_params=_params(("parallel", "parallel", "arbitrary"), 40),
        name="attention",
    )(proj_b, kv, kv)


def _gated_out_norm_kernel(o_ref, g_ref, w_ref, x_ref, fn_ref, out_ref):
    a = (o_ref[...].astype(jnp.float32) * _silu(g_ref[...].astype(jnp.float32))).astype(jnp.bfloat16)
    y = x_ref[...] + jnp.dot(a, w_ref[...], preferred_element_type=jnp.float32)
    ms = jnp.mean(y * y, axis=-1, keepdims=True)
    out_ref[...] = y * lax.rsqrt(ms + EPS) * fn_ref[...]


def _gated_out_norm(o, proj_b, w, x1, fn, *, tm=256):
    m, d = x1.shape
    return pl.pallas_call(
        _gated_out_norm_kernel,
        grid=(m // tm,),
        in_specs=[
            pl.BlockSpec((tm, d), lambda i: (i, 0)),
            pl.BlockSpec((tm, d), lambda i: (i, 1)),
            pl.BlockSpec((d, d), lambda i: (0, 0)),
            pl.BlockSpec((tm, d), lambda i: (i, 0)),
            pl.BlockSpec((1, d), lambda i: (0, 0)),
        ],
        out_specs=pl.BlockSpec((tm, d), lambda i: (i, 0)),
        out_shape=jax.ShapeDtypeStruct((m, d), jnp.float32),
        compiler_params=_params(("parallel",), 48),
        name="gated_out_norm",
    )(o, proj_b, w, x1, fn)


def kernel(x, a_norm, a_w_in, a_conv_w, a_conv_b, a_w_r, a_b_r, a_w_i, a_b_i, a_lambda,
           a_w_out, kv_norm, w_kv, b_norm, b_w_in, b_w_out, final_norm):
    bsz, s, d = x.shape
    m = bsz * s
    bf = jnp.bfloat16
    x2 = x.reshape(m, d)

    proj_a = _norm_matmul(x2, a_norm[0:1], a_w_in[0].astype(bf), jnp.float32,
                          name="norm_matmul_a")
    c = proj_a.shape[1] // 2
    yg = _lru(proj_a.reshape(bsz, s, 2 * c), a_conv_w[0], a_conv_b[0:1],
              a_w_r[0].astype(bf), a_b_r[0:1], a_w_i[0].astype(bf), a_b_i[0:1],
              a_lambda[0:1])
    x1 = _matmul_residual(yg.reshape(m, c), a_w_out[0].astype(bf), x2)

    kv = _norm_matmul(x1, kv_norm.reshape(1, d), w_kv.astype(bf), bf, name="norm_matmul_kv")
    proj_b = _norm_matmul(x1, b_norm[0:1], b_w_in[0].astype(bf), bf, name="norm_matmul_b")
    da = proj_b.shape[1] // 2
    o = _attention(proj_b.reshape(bsz, s, 2 * da), kv.reshape(bsz, s, 2 * da))
    out = _gated_out_norm(o.reshape(m, da), proj_b, b_w_out[0].astype(bf), x1,
                          final_norm.reshape(1, d))
    return out.reshape(bsz, s, d)
```

```python
import functools
import math

import jax
import jax.numpy as jnp
from jax import lax
from jax.experimental import pallas as pl
from jax.experimental.pallas import tpu as pltpu

EPS = 1e-6
LOG2E = 1.0 / math.log(2.0)
LRU_C = 8.0
CONV_W = 4
LRU_BW = 256
N_HEADS = 16
HEAD_DIM = 128
SKIP_BITS = 200.0
DONE_BITS = 1.0e6

SUBLANES = 8
LANES = 128
SEG = SUBLANES
PAD = 4

MIB = 1024 * 1024


def _params(semantics, vmem_mib):
    return pltpu.CompilerParams(dimension_semantics=semantics,
                                vmem_limit_bytes=vmem_mib * MIB)


def _silu(g):
    return g * jax.nn.sigmoid(g)


def _matmul_kernel(a_ref, w_ref, o_ref):
    o_ref[...] = jnp.dot(a_ref[...], w_ref[...],
                         preferred_element_type=jnp.float32).astype(o_ref.dtype)


def _matmul(a, w, out_dtype, *, tm=1024, tn=2048, name):
    m, k = a.shape
    n = w.shape[1]
    return pl.pallas_call(
        _matmul_kernel,
        grid=(m // tm, n // tn),
        in_specs=[
            pl.BlockSpec((tm, k), lambda i, j: (i, 0)),
            pl.BlockSpec((k, tn), lambda i, j: (0, j)),
        ],
        out_specs=pl.BlockSpec((tm, tn), lambda i, j: (i, j)),
        out_shape=jax.ShapeDtypeStruct((m, n), out_dtype),
        compiler_params=_params(("parallel", "parallel"), 40),
        name=name,
    )(a, w)


def _lru_kernel(x_ref, g_ref, wx_ref, wg_ref, cw_ref, cb_ref, wr_ref, br_ref, wi_ref, bi_ref,
                lam_ref, o_ref, h_s, g_s, xext, a_s, u_s, hl_s, al_s, y_s, carry_s):
    t = pl.program_id(2)
    tt = x_ref.shape[0]
    cw = wx_ref.shape[1]
    ns = cw // LANES
    seg = tt // SEG
    pitch = seg + PAD

    x = x_ref[...]
    ms = jnp.mean(x * x, axis=-1, keepdims=True)
    h_s[...] = (x * lax.rsqrt(ms + EPS) * g_ref[...]).astype(h_s.dtype)

    @pl.when(t == 0)
    def _():
        xext[:, 0:SUBLANES, :] = jnp.zeros((ns, SUBLANES, LANES), jnp.float32)
        carry_s[...] = jnp.zeros_like(carry_s)

    @pl.when(t > 0)
    def _():
        xext[:, 0:SUBLANES, :] = xext[:, tt:tt + SUBLANES, :]

    def lanes(s):
        return slice(s * LANES, (s + 1) * LANES)

    nblk = cw // LRU_BW

    def project_x(n):
        xb = jnp.dot(h_s[...], wx_ref[:, n * LRU_BW:(n + 1) * LRU_BW],
                     preferred_element_type=jnp.float32)
        for j, s in enumerate(range(n * LRU_BW // LANES, (n + 1) * LRU_BW // LANES)):
            xext[s, SUBLANES:SUBLANES + tt, :] = xb[:, lanes(j)]

    def project_gate(n):
        g_s[:, n * LRU_BW:(n + 1) * LRU_BW] = jnp.dot(
            h_s[...], wg_ref[:, n * LRU_BW:(n + 1) * LRU_BW],
            preferred_element_type=jnp.float32)

    project_x(0)
    for n in range(nblk):
        slabs = range(n * LRU_BW // LANES, (n + 1) * LRU_BW // LANES)
        if n + 1 < nblk:
            project_x(n + 1)
        else:
            project_gate(0)
        parts = []
        for s in slabs:
            xc = cb_ref[:, lanes(s)]
            for w in range(CONV_W):
                off = SUBLANES - (CONV_W - 1) + w
                xc = xc + cw_ref[w:w + 1, lanes(s)] * xext[s, off:off + tt, :]
            parts.append(xc)
        xc = jnp.concatenate(parts, axis=1)
        sl = slice(n * LRU_BW, (n + 1) * LRU_BW)
        xn = xc.astype(jnp.bfloat16)
        r = jax.nn.sigmoid(
            jnp.dot(xn, wr_ref[n], preferred_element_type=jnp.float32) + br_ref[:, sl])
        i = jax.nn.sigmoid(
            jnp.dot(xn, wi_ref[n], preferred_element_type=jnp.float32) + bi_ref[:, sl])
        neg_lam = -lam_ref[:, sl]
        softplus = jnp.maximum(neg_lam, 0.0) + jnp.log1p(jnp.exp(-jnp.abs(neg_lam)))
        log_a = r * (-LRU_C * softplus)
        a = jnp.exp(log_a)
        one_minus_a2 = -jnp.tanh(log_a) * (a * a + 1.0)
        mult = jnp.where(one_minus_a2 > 0.0, one_minus_a2 * lax.rsqrt(one_minus_a2), 0.0)
        u = mult * (i * xc)
        for j, s in enumerate(slabs):
            for g in range(SEG):
                a_s[s, g * pitch:g * pitch + seg, :] = a[g * seg:(g + 1) * seg, lanes(j)]
                u_s[s, g * pitch:g * pitch + seg, :] = u[g * seg:(g + 1) * seg, lanes(j)]

    for n in range(1, nblk):
        project_gate(n)

    for s in range(ns):
        h = jnp.zeros((SEG, LANES), jnp.float32)
        prod = jnp.ones((SEG, LANES), jnp.float32)
        for p in range(seg):
            ap = a_s[s, pl.ds(p, SEG, stride=pitch), :]
            h = ap * h + u_s[s, pl.ds(p, SEG, stride=pitch), :]
            prod = ap * prod
            hl_s[s, p] = h
            al_s[s, p] = prod
        entering = [carry_s[s]]
        for g in range(SEG - 1):
            entering.append(prod[g:g + 1] * entering[-1] + h[g:g + 1])
        carry_s[s] = prod[SEG - 1:SEG] * entering[-1] + h[SEG - 1:SEG]
        h_in = jnp.concatenate(entering, axis=0)
        for p in range(seg):
            y_s[s, pl.ds(p, SEG, stride=pitch), :] = hl_s[s, p] + al_s[s, p] * h_in

    for s in range(ns):
        for g in range(SEG):
            rows = slice(g * seg, (g + 1) * seg)
            y = y_s[s, g * pitch:g * pitch + seg, :]
            o_ref[rows, lanes(s)] = (y * _silu(g_s[rows, lanes(s)])).astype(o_ref.dtype)


def _lru(x, g, w_in, conv_w, conv_b, w_r, b_r, w_i, b_i, lam, *, tt=512, cw=1280):
    bsz, s, d = x.shape
    c = w_in.shape[1] // 2
    nc = c // cw
    nb = cw // LRU_BW
    ns = cw // LANES
    seg = tt // SEG
    row = lambda b, ci, t: (0, ci)
    return pl.pallas_call(
        _lru_kernel,
        grid=(bsz, nc, s // tt),
        in_specs=[
            pl.BlockSpec((None, tt, d), lambda b, ci, t: (b, t, 0)),
            pl.BlockSpec((1, d), lambda b, ci, t: (0, 0)),
            pl.BlockSpec((d, cw), lambda b, ci, t: (0, ci), pipeline_mode=pl.Buffered(1)),
            pl.BlockSpec((d, cw), lambda b, ci, t: (0, nc + ci), pipeline_mode=pl.Buffered(1)),
            pl.BlockSpec((CONV_W, cw), row),
            pl.BlockSpec((1, cw), row),
            pl.BlockSpec((nb, LRU_BW, LRU_BW), lambda b, ci, t: (ci, 0, 0)),
            pl.BlockSpec((1, cw), row),
            pl.BlockSpec((nb, LRU_BW, LRU_BW), lambda b, ci, t: (ci, 0, 0)),
            pl.BlockSpec((1, cw), row),
            pl.BlockSpec((1, cw), row),
        ],
        out_specs=pl.BlockSpec((None, tt, cw), lambda b, ci, t: (b, t, ci)),
        out_shape=jax.ShapeDtypeStruct((bsz, s, c), jnp.bfloat16),
        scratch_shapes=[
            pltpu.VMEM((tt, d), jnp.bfloat16),
            pltpu.VMEM((tt, cw), jnp.float32),
            pltpu.VMEM((ns, tt + SUBLANES, LANES), jnp.float32),
            pltpu.VMEM((ns, SEG * (seg + PAD), LANES), jnp.float32),
            pltpu.VMEM((ns, SEG * (seg + PAD), LANES), jnp.float32),
            pltpu.VMEM((ns, seg, SEG, LANES), jnp.float32),
            pltpu.VMEM((ns, seg, SEG, LANES), jnp.float32),
            pltpu.VMEM((ns, SEG * (seg + PAD), LANES), jnp.float32),
            pltpu.VMEM((ns, 1, LANES), jnp.float32),
        ],
        compiler_params=_params(("parallel", "parallel", "arbitrary"), 52),
        name="lru",
    )(x, g, w_in, w_in, conv_w, conv_b, w_r, b_r, w_i, b_i, lam)


def _residual_norms_kernel(a_ref, w_ref, r_ref, g1_ref, g2_ref, x1_ref, h1_ref, h2_ref):
    x1 = r_ref[...] + jnp.dot(a_ref[...], w_ref[...], preferred_element_type=jnp.float32)
    x1_ref[...] = x1
    xn = x1 * lax.rsqrt(jnp.mean(x1 * x1, axis=-1, keepdims=True) + EPS)
    h1_ref[...] = (xn * g1_ref[...]).astype(h1_ref.dtype)
    h2_ref[...] = (xn * g2_ref[...]).astype(h2_ref.dtype)


def _residual_norms(a, w, res, g1, g2, *, tm=512):
    m, k = a.shape
    d = w.shape[1]
    rows = pl.BlockSpec((tm, d), lambda i: (i, 0))
    gain = pl.BlockSpec((1, d), lambda i: (0, 0))
    return pl.pallas_call(
        _residual_norms_kernel,
        grid=(m // tm,),
        in_specs=[
            pl.BlockSpec((tm, k), lambda i: (i, 0)),
            pl.BlockSpec((k, d), lambda i: (0, 0), pipeline_mode=pl.Buffered(1)),
            rows, gain, gain,
        ],
        out_specs=[rows, rows, rows],
        out_shape=[jax.ShapeDtypeStruct((m, d), jnp.float32),
                   jax.ShapeDtypeStruct((m, d), jnp.bfloat16),
                   jax.ShapeDtypeStruct((m, d), jnp.bfloat16)],
        compiler_params=_params(("parallel",), 48),
        name="residual_norms",
    )(a, w, res, g1, g2)


def _attention_kernel(q_ref, k_ref, v_ref, o_ref, q_s, z_s, w_s, c_ref, acc_ref, *, tk):
    i = pl.program_id(2)
    tq, dh = q_ref.shape
    nsub = tq // tk
    q_s[...] = (q_ref[...].astype(jnp.float32) * (LOG2E / math.sqrt(dh))).astype(jnp.bfloat16)

    rows = lax.broadcasted_iota(jnp.int32, (tk, tk), 0)
    cols = lax.broadcasted_iota(jnp.int32, (tk, tk), 1)
    tri = (rows >= cols).astype(jnp.bfloat16)

    c_ref[...] = jnp.zeros_like(c_ref)
    acc_ref[...] = jnp.zeros_like(acc_ref)

    def sub(a):
        return pl.ds(a * tk, tk)

    def key_block(d, a):
        blk = i * nsub + a - d
        return pl.ds(pl.multiple_of(jnp.maximum(blk, 0) * tk, tk), tk), blk >= 0

    def scores(q, kb):
        return lax.dot_general(q, kb, (((1,), (1,)), ((), ())),
                               preferred_element_type=jnp.float32)

    def weights(z, c, mask):
        sp = jnp.maximum(z, 0.0) + jnp.log(1.0 + jnp.exp2(-jnp.abs(z))) * LOG2E
        if mask is not None:
            sp = jnp.where(mask, sp, 0.0)
        incl = jnp.dot(sp.astype(jnp.bfloat16), tri, preferred_element_type=jnp.float32)
        w = jnp.exp2(z - incl - c)
        if mask is not None:
            w = jnp.where(mask, w, 0.0)
        return w.astype(jnp.bfloat16), incl[:, 0:1]

    def stage_scores(d, par):
        for a in range(nsub):
            keys, _ = key_block(d, a)
            z_s[par, sub(a), :] = scores(q_s[sub(a), :], k_ref[keys, :])

    def stage_weights(d, par, diagonal=False):
        if diagonal:
            row = lax.broadcasted_iota(jnp.int32, (tq, tk), 0) & (tk - 1)
            mask = lax.broadcasted_iota(jnp.int32, (tq, tk), 1) < row
        else:
            mask = None
            for a in range(nsub):
                _, exists = key_block(d, a)
                c_ref[sub(a), :] += jnp.where(exists, 0.0, DONE_BITS)
        w, total = weights(z_s[par], c_ref[...], mask)
        w_s[par] = w
        c_ref[...] += total

    def stage_values(d, par):
        for a in range(nsub):
            keys, _ = key_block(d, a)
            acc_ref[sub(a), :] += jnp.dot(w_s[par, sub(a), :], v_ref[keys, :],
                                          preferred_element_type=jnp.float32)

    stage_scores(0, 1)
    stage_scores(1, 0)
    stage_weights(0, 1, diagonal=True)
    stage_values(0, 1)
    stage_weights(1, 0)
    c_min1 = jnp.min(c_ref[...])

    @pl.when(c_min1 < SKIP_BITS)
    def _():
        stage_scores(2, 1)

    def body(carry):
        m, _ = carry
        d = 2 * m + 2
        stage_scores(d + 1, 0)
        stage_values(d - 1, 0)
        stage_weights(d, 1)
        c_min = jnp.min(c_ref[...])
        stage_scores(d + 2, 1)
        stage_values(d, 1)
        stage_weights(d + 1, 0)
        return m + 1, c_min

    def cond(carry):
        m, c_min = carry
        return jnp.logical_and(m < n_pairs, c_min < SKIP_BITS)

    n_pairs = (i * nsub + nsub - 2) // 2
    m_done, _ = lax.while_loop(cond, body, (jnp.int32(0), c_min1))
    stage_values(2 * m_done + 1, 0)
    o_ref[...] = acc_ref[...].astype(o_ref.dtype)


def _attention(proj_b, kv, *, tq=2048, tk=256):
    bsz, s, _ = proj_b.shape
    kern = functools.partial(_attention_kernel, tk=tk)
    return pl.pallas_call(
        kern,
        grid=(bsz, N_HEADS, s // tq),
        in_specs=[
            pl.BlockSpec((None, tq, HEAD_DIM), lambda b, h, i: (b, i, h)),
            pl.BlockSpec((None, s, HEAD_DIM), lambda b, h, i: (b, 0, h)),
            pl.BlockSpec((None, s, HEAD_DIM), lambda b, h, i: (b, 0, N_HEADS + h)),
        ],
        out_specs=pl.BlockSpec((None, tq, HEAD_DIM), lambda b, h, i: (b, i, h)),
        out_shape=jax.ShapeDtypeStruct((bsz, s, N_HEADS * HEAD_DIM), jnp.bfloat16),
        scratch_shapes=[
            pltpu.VMEM((tq, HEAD_DIM), jnp.bfloat16),
            pltpu.VMEM((2, tq, tk), jnp.float32),
            pltpu.VMEM((2, tq, tk), jnp.bfloat16),
            pltpu.VMEM((tq, 1), jnp.float32),
            pltpu.VMEM((tq, HEAD_DIM), jnp.float32),
        ],
        compiler_params=_params(("parallel", "parallel", "arbitrary"), 40),
        name="attention",
    )(proj_b, kv, kv)


def _gated_out_norm_kernel(o_ref, g_ref, w_ref, x_ref, fn_ref, out_ref):
    a = (o_ref[...].astype(jnp.float32) * _silu(g_ref[...].astype(jnp.float32))).astype(jnp.bfloat16)
    y = x_ref[...] + jnp.dot(a, w_ref[...], preferred_element_type=jnp.float32)
    ms = jnp.mean(y * y, axis=-1, keepdims=True)
    out_ref[...] = y * lax.rsqrt(ms + EPS) * fn_ref[...]


def _gated_out_norm(o, proj_b, w, x1, fn, *, tm=512):
    m, d = x1.shape
    return pl.pallas_call(
        _gated_out_norm_kernel,
        grid=(m // tm,),
        in_specs=[
            pl.BlockSpec((tm, d), lambda i: (i, 0)),
            pl.BlockSpec((tm, d), lambda i: (i, 1)),
            pl.BlockSpec((d, d), lambda i: (0, 0), pipeline_mode=pl.Buffered(1)),
            pl.BlockSpec((tm, d), lambda i: (i, 0)),
            pl.BlockSpec((1, d), lambda i: (0, 0)),
        ],
        out_specs=pl.BlockSpec((tm, d), lambda i: (i, 0)),
        out_shape=jax.ShapeDtypeStruct((m, d), jnp.float32),
        compiler_params=_params(("parallel",), 48),
        name="gated_out_norm",
    )(o, proj_b, w, x1, fn)


def kernel(x, a_norm, a_w_in, a_conv_w, a_conv_b, a_w_r, a_b_r, a_w_i, a_b_i, a_lambda,
           a_w_out, kv_norm, w_kv, b_norm, b_w_in, b_w_out, final_norm):
    bsz, s, d = x.shape
    m = bsz * s
    bf = jnp.bfloat16
    x2 = x.reshape(m, d)

    c = a_w_in.shape[2] // 2
    yg = _lru(x, a_norm[0:1], a_w_in[0].astype(bf), a_conv_w[0], a_conv_b[0:1],
              a_w_r[0].astype(bf), a_b_r[0:1], a_w_i[0].astype(bf), a_b_i[0:1],
              a_lambda[0:1])
    x1, h_kv, h_b = _residual_norms(yg.reshape(m, c), a_w_out[0].astype(bf), x2,
                                    kv_norm.reshape(1, d), b_norm[0:1])
    kv = _matmul(h_kv, w_kv.astype(bf), bf, name="matmul_kv")
    proj_b = _matmul(h_b, b_w_in[0].astype(bf), bf, name="matmul_b")
    da = proj_b.shape[1] // 2
    o = _attention(proj_b.reshape(bsz, s, 2 * da), kv.reshape(bsz, s, 2 * da))
    out = _gated_out_norm(o.reshape(m, da), proj_b, b_w_out[0].astype(bf), x1,
                          final_norm.reshape(1, d))
    return out.reshape(bsz, s, d)
```

```python
import functools
import math

import jax
import jax.numpy as jnp
from jax import lax
from jax.experimental import pallas as pl
from jax.experimental.pallas import tpu as pltpu

EPS = 1e-6
LOG2E = 1.0 / math.log(2.0)
LRU_C = 8.0
CONV_W = 4
LRU_BW = 256
N_HEADS = 16
HEAD_DIM = 128
SKIP_BITS = 200.0
DONE_BITS = 1.0e6

SUBLANES = 8
LANES = 128
SEG = SUBLANES
PAD = 4

MIB = 1024 * 1024


def _params(semantics, vmem_mib):
    return pltpu.CompilerParams(dimension_semantics=semantics,
                                vmem_limit_bytes=vmem_mib * MIB)


def _silu(g):
    return g * jax.nn.sigmoid(g)


def _matmul_kernel(a_ref, w_ref, o_ref):
    o_ref[...] = jnp.dot(a_ref[...], w_ref[...],
                         preferred_element_type=jnp.float32).astype(o_ref.dtype)


def _matmul(a, w, out_dtype, *, tm=1024, tn=2048, name):
    m, k = a.shape
    n = w.shape[1]
    return pl.pallas_call(
        _matmul_kernel,
        grid=(m // tm, n // tn),
        in_specs=[
            pl.BlockSpec((tm, k), lambda i, j: (i, 0)),
            pl.BlockSpec((k, tn), lambda i, j: (0, j)),
        ],
        out_specs=pl.BlockSpec((tm, tn), lambda i, j: (i, j)),
        out_shape=jax.ShapeDtypeStruct((m, n), out_dtype),
        compiler_params=_params(("parallel", "parallel"), 40),
        name=name,
    )(a, w)


def _lru_kernel(x_ref, g_ref, wx_ref, wg_ref, cw_ref, cb_ref, wr_ref, br_ref, wi_ref, bi_ref,
                lam_ref, o_ref, h_s, g_s, xext, a_s, u_s, hl_s, al_s, y_s, carry_s):
    t = pl.program_id(2)
    tt = x_ref.shape[0]
    cw = wx_ref.shape[1]
    ns = cw // LANES
    seg = tt // SEG
    pitch = seg + PAD

    x = x_ref[...]
    ms = jnp.mean(x * x, axis=-1, keepdims=True)
    h_s[...] = (x * lax.rsqrt(ms + EPS) * g_ref[...]).astype(h_s.dtype)

    @pl.when(t == 0)
    def _():
        xext[:, 0:SUBLANES, :] = jnp.zeros((ns, SUBLANES, LANES), jnp.float32)
        carry_s[...] = jnp.zeros_like(carry_s)

    @pl.when(t > 0)
    def _():
        xext[:, 0:SUBLANES, :] = xext[:, tt:tt + SUBLANES, :]

    def lanes(s):
        return slice(s * LANES, (s + 1) * LANES)

    nblk = cw // LRU_BW

    def project_x(n):
        xb = jnp.dot(h_s[...], wx_ref[:, n * LRU_BW:(n + 1) * LRU_BW],
                     preferred_element_type=jnp.float32)
        for j, s in enumerate(range(n * LRU_BW // LANES, (n + 1) * LRU_BW // LANES)):
            xext[s, SUBLANES:SUBLANES + tt, :] = xb[:, lanes(j)]

    def project_gate(n):
        g_s[:, n * LRU_BW:(n + 1) * LRU_BW] = jnp.dot(
            h_s[...], wg_ref[:, n * LRU_BW:(n + 1) * LRU_BW],
            preferred_element_type=jnp.float32)

    project_x(0)
    for n in range(nblk):
        slabs = range(n * LRU_BW // LANES, (n + 1) * LRU_BW // LANES)
        if n + 1 < nblk:
            project_x(n + 1)
        else:
            project_gate(0)
        parts = []
        for s in slabs:
            xc = cb_ref[:, lanes(s)]
            for w in range(CONV_W):
                off = SUBLANES - (CONV_W - 1) + w
                xc = xc + cw_ref[w:w + 1, lanes(s)] * xext[s, off:off + tt, :]
            parts.append(xc)
        xc = jnp.concatenate(parts, axis=1)
        sl = slice(n * LRU_BW, (n + 1) * LRU_BW)
        xn = xc.astype(jnp.bfloat16)
        r = jax.nn.sigmoid(
            jnp.dot(xn, wr_ref[n], preferred_element_type=jnp.float32) + br_ref[:, sl])
        i = jax.nn.sigmoid(
            jnp.dot(xn, wi_ref[n], preferred_element_type=jnp.float32) + bi_ref[:, sl])
        neg_lam = -lam_ref[:, sl]
        softplus = jnp.maximum(neg_lam, 0.0) + jnp.log1p(jnp.exp(-jnp.abs(neg_lam)))
        log_a = r * (-LRU_C * softplus)
        a = jnp.exp(log_a)
        one_minus_a2 = -jnp.tanh(log_a) * (a * a + 1.0)
        mult = jnp.where(one_minus_a2 > 0.0, one_minus_a2 * lax.rsqrt(one_minus_a2), 0.0)
        u = mult * (i * xc)
        for j, s in enumerate(slabs):
            for g in range(SEG):
                a_s[s, g * pitch:g * pitch + seg, :] = a[g * seg:(g + 1) * seg, lanes(j)]
                u_s[s, g * pitch:g * pitch + seg, :] = u[g * seg:(g + 1) * seg, lanes(j)]

    for n in range(1, nblk):
        project_gate(n)

    for s in range(ns):
        h = jnp.zeros((SEG, LANES), jnp.float32)
        prod = jnp.ones((SEG, LANES), jnp.float32)
        for p in range(seg):
            ap = a_s[s, pl.ds(p, SEG, stride=pitch), :]
            h = ap * h + u_s[s, pl.ds(p, SEG, stride=pitch), :]
            prod = ap * prod
            hl_s[s, p] = h
            al_s[s, p] = prod
        entering = [carry_s[s]]
        for g in range(SEG - 1):
            entering.append(prod[g:g + 1] * entering[-1] + h[g:g + 1])
        carry_s[s] = prod[SEG - 1:SEG] * entering[-1] + h[SEG - 1:SEG]
        h_in = jnp.concatenate(entering, axis=0)
        for p in range(seg):
            y_s[s, pl.ds(p, SEG, stride=pitch), :] = hl_s[s, p] + al_s[s, p] * h_in

    for s in range(ns):
        for g in range(SEG):
            rows = slice(g * seg, (g + 1) * seg)
            y = y_s[s, g * pitch:g * pitch + seg, :]
            o_ref[rows, lanes(s)] = (y * _silu(g_s[rows, lanes(s)])).astype(o_ref.dtype)


def _lru(x, g, w_in, conv_w, conv_b, w_r, b_r, w_i, b_i, lam, *, tt=512, cw=1280):
    bsz, s, d = x.shape
    c = w_in.shape[1] // 2
    nc = c // cw
    nb = cw // LRU_BW
    ns = cw // LANES
    seg = tt // SEG
    row = lambda b, ci, t: (0, ci)
    return pl.pallas_call(
        _lru_kernel,
        grid=(bsz, nc, s // tt),
        in_specs=[
            pl.BlockSpec((None, tt, d), lambda b, ci, t: (b, t, 0)),
            pl.BlockSpec((1, d), lambda b, ci, t: (0, 0)),
            pl.BlockSpec((d, cw), lambda b, ci, t: (0, ci), pipeline_mode=pl.Buffered(1)),
            pl.BlockSpec((d, cw), lambda b, ci, t: (0, nc + ci), pipeline_mode=pl.Buffered(1)),
            pl.BlockSpec((CONV_W, cw), row),
            pl.BlockSpec((1, cw), row),
            pl.BlockSpec((nb, LRU_BW, LRU_BW), lambda b, ci, t: (ci, 0, 0)),
            pl.BlockSpec((1, cw), row),
            pl.BlockSpec((nb, LRU_BW, LRU_BW), lambda b, ci, t: (ci, 0, 0)),
            pl.BlockSpec((1, cw), row),
            pl.BlockSpec((1, cw), row),
        ],
        out_specs=pl.BlockSpec((None, tt, cw), lambda b, ci, t: (b, t, ci)),
        out_shape=jax.ShapeDtypeStruct((bsz, s, c), jnp.bfloat16),
        scratch_shapes=[
            pltpu.VMEM((tt, d), jnp.bfloat16),
            pltpu.VMEM((tt, cw), jnp.float32),
            pltpu.VMEM((ns, tt + SUBLANES, LANES), jnp.float32),
            pltpu.VMEM((ns, SEG * (seg + PAD), LANES), jnp.float32),
            pltpu.VMEM((ns, SEG * (seg + PAD), LANES), jnp.float32),
            pltpu.VMEM((ns, seg, SEG, LANES), jnp.float32),
            pltpu.VMEM((ns, seg, SEG, LANES), jnp.float32),
            pltpu.VMEM((ns, SEG * (seg + PAD), LANES), jnp.float32),
            pltpu.VMEM((ns, 1, LANES), jnp.float32),
        ],
        compiler_params=_params(("parallel", "parallel", "arbitrary"), 52),
        name="lru",
    )(x, g, w_in, w_in, conv_w, conv_b, w_r, b_r, w_i, b_i, lam)


def _residual_norms_kernel(a_ref, w_ref, r_ref, g1_ref, g2_ref, x1_ref, h1_ref, h2_ref):
    x1 = r_ref[...] + jnp.dot(a_ref[...], w_ref[...], preferred_element_type=jnp.float32)
    x1_ref[...] = x1
    xn = x1 * lax.rsqrt(jnp.mean(x1 * x1, axis=-1, keepdims=True) + EPS)
    h1_ref[...] = (xn * g1_ref[...]).astype(h1_ref.dtype)
    h2_ref[...] = (xn * g2_ref[...]).astype(h2_ref.dtype)


def _residual_norms(a, w, res, g1, g2, *, tm=512):
    m, k = a.shape
    d = w.shape[1]
    rows = pl.BlockSpec((tm, d), lambda i: (i, 0))
    gain = pl.BlockSpec((1, d), lambda i: (0, 0))
    return pl.pallas_call(
        _residual_norms_kernel,
        grid=(m // tm,),
        in_specs=[
            pl.BlockSpec((tm, k), lambda i: (i, 0)),
            pl.BlockSpec((k, d), lambda i: (0, 0), pipeline_mode=pl.Buffered(1)),
            rows, gain, gain,
        ],
        out_specs=[rows, rows, rows],
        out_shape=[jax.ShapeDtypeStruct((m, d), jnp.float32),
                   jax.ShapeDtypeStruct((m, d), jnp.bfloat16),
                   jax.ShapeDtypeStruct((m, d), jnp.bfloat16)],
        compiler_params=_params(("parallel",), 48),
        name="residual_norms",
    )(a, w, res, g1, g2)


def _attention_kernel(q_ref, k_ref, v_ref, o_ref, q_s, z_s, w_s, c_ref, acc_ref, *, tk):
    i = pl.program_id(2)
    tq, dh = q_ref.shape
    nsub = tq // tk
    q_s[...] = (q_ref[...].astype(jnp.float32) * (LOG2E / math.sqrt(dh))).astype(jnp.bfloat16)

    rows = lax.broadcasted_iota(jnp.int32, (tk, tk), 0)
    cols = lax.broadcasted_iota(jnp.int32, (tk, tk), 1)
    tri = (rows >= cols).astype(jnp.bfloat16)

    c_ref[...] = jnp.zeros_like(c_ref)
    acc_ref[...] = jnp.zeros_like(acc_ref)

    def sub(a):
        return pl.ds(a * tk, tk)

    def key_block(d, a):
        blk = i * nsub + a - d
        return pl.ds(pl.multiple_of(jnp.maximum(blk, 0) * tk, tk), tk), blk >= 0

    def scores(q, kb):
        return lax.dot_general(q, kb, (((1,), (1,)), ((), ())),
                               preferred_element_type=jnp.float32)

    def weights(z, c, mask):
        sp = jnp.maximum(z, 0.0) + jnp.log(1.0 + jnp.exp2(-jnp.abs(z))) * LOG2E
        if mask is not None:
            sp = jnp.where(mask, sp, 0.0)
        incl = jnp.dot(sp.astype(jnp.bfloat16), tri, preferred_element_type=jnp.float32)
        w = jnp.exp2(z - incl - c)
        if mask is not None:
            w = jnp.where(mask, w, 0.0)
        return w.astype(jnp.bfloat16), incl[:, 0:1]

    def stage_scores(d, par):
        for a in range(nsub):
            keys, _ = key_block(d, a)
            z_s[par, sub(a), :] = scores(q_s[sub(a), :], k_ref[keys, :])

    def stage_weights(d, par, diagonal=False):
        if diagonal:
            row = lax.broadcasted_iota(jnp.int32, (tq, tk), 0) & (tk - 1)
            mask = lax.broadcasted_iota(jnp.int32, (tq, tk), 1) < row
        else:
            mask = None
            for a in range(nsub):
                _, exists = key_block(d, a)
                c_ref[sub(a), :] += jnp.where(exists, 0.0, DONE_BITS)
        w, total = weights(z_s[par], c_ref[...], mask)
        w_s[par] = w
        c_ref[...] += total

    def stage_values(d, par):
        for a in range(nsub):
            keys, _ = key_block(d, a)
            acc_ref[sub(a), :] += jnp.dot(w_s[par, sub(a), :], v_ref[keys, :],
                                          preferred_element_type=jnp.float32)

    stage_scores(0, 1)
    stage_scores(1, 0)
    stage_weights(0, 1, diagonal=True)
    stage_values(0, 1)
    stage_weights(1, 0)
    c_min1 = jnp.min(c_ref[...])

    @pl.when(c_min1 < SKIP_BITS)
    def _():
        stage_scores(2, 1)

    def body(carry):
        m, _ = carry
        d = 2 * m + 2
        stage_scores(d + 1, 0)
        stage_values(d - 1, 0)
        stage_weights(d, 1)
        c_min = jnp.min(c_ref[...])
        stage_scores(d + 2, 1)
        stage_values(d, 1)
        stage_weights(d + 1, 0)
        return m + 1, c_min

    def cond(carry):
        m, c_min = carry
        return jnp.logical_and(m < n_pairs, c_min < SKIP_BITS)

    n_pairs = (i * nsub + nsub - 2) // 2
    m_done, _ = lax.while_loop(cond, body, (jnp.int32(0), c_min1))
    stage_values(2 * m_done + 1, 0)
    o_ref[...] = acc_ref[...].astype(o_ref.dtype)


def _attention(proj_b, kv, *, tq=4096, tk=256):
    bsz, s, _ = proj_b.shape
    kern = functools.partial(_attention_kernel, tk=tk)
    return pl.pallas_call(
        kern,
        grid=(bsz, N_HEADS, s // tq),
        in_specs=[
            pl.BlockSpec((None, tq, HEAD_DIM), lambda b, h, i: (b, i, h)),
            pl.BlockSpec((None, s, HEAD_DIM), lambda b, h, i: (b, 0, h)),
            pl.BlockSpec((None, s, HEAD_DIM), lambda b, h, i: (b, 0, N_HEADS + h)),
        ],
        out_specs=pl.BlockSpec((None, tq, HEAD_DIM), lambda b, h, i: (b, i, h)),
        out_shape=jax.ShapeDtypeStruct((bsz, s, N_HEADS * HEAD_DIM), jnp.bfloat16),
        scratch_shapes=[
            pltpu.VMEM((tq, HEAD_DIM), jnp.bfloat16),
            pltpu.VMEM((2, tq, tk), jnp.float32),
            pltpu.VMEM((2, tq, tk), jnp.bfloat16),
            pltpu.VMEM((tq, 1), jnp.float32),
            pltpu.VMEM((tq, HEAD_DIM), jnp.float32),
        ],
        compiler_params=_params(("parallel", "parallel", "arbitrary"), 40),
        name="attention",
    )(proj_b, kv, kv)


def _gated_out_norm_kernel(o_ref, g_ref, w_ref, x_ref, fn_ref, out_ref):
    a = (o_ref[...].astype(jnp.float32) * _silu(g_ref[...].astype(jnp.float32))).astype(jnp.bfloat16)
    y = x_ref[...] + jnp.dot(a, w_ref[...], preferred_element_type=jnp.float32)
    ms = jnp.mean(y * y, axis=-1, keepdims=True)
    out_ref[...] = y * lax.rsqrt(ms + EPS) * fn_ref[...]


def _gated_out_norm(o, proj_b, w, x1, fn, *, tm=512):
    m, d = x1.shape
    return pl.pallas_call(
        _gated_out_norm_kernel,
        grid=(m // tm,),
        in_specs=[
            pl.BlockSpec((tm, d), lambda i: (i, 0)),
            pl.BlockSpec((tm, d), lambda i: (i, 1)),
            pl.BlockSpec((d, d), lambda i: (0, 0), pipeline_mode=pl.Buffered(1)),
            pl.BlockSpec((tm, d), lambda i: (i, 0)),
            pl.BlockSpec((1, d), lambda i: (0, 0)),
        ],
        out_specs=pl.BlockSpec((tm, d), lambda i: (i, 0)),
        out_shape=jax.ShapeDtypeStruct((m, d), jnp.float32),
        compiler_params=_params(("parallel",), 48),
        name="gated_out_norm",
    )(o, proj_b, w, x1, fn)


def kernel(x, a_norm, a_w_in, a_conv_w, a_conv_b, a_w_r, a_b_r, a_w_i, a_b_i, a_lambda,
           a_w_out, kv_norm, w_kv, b_norm, b_w_in, b_w_out, final_norm):
    bsz, s, d = x.shape
    m = bsz * s
    bf = jnp.bfloat16
    x2 = x.reshape(m, d)

    c = a_w_in.shape[2] // 2
    yg = _lru(x, a_norm[0:1], a_w_in[0].astype(bf), a_conv_w[0], a_conv_b[0:1],
              a_w_r[0].astype(bf), a_b_r[0:1], a_w_i[0].astype(bf), a_b_i[0:1],
              a_lambda[0:1])
    x1, h_kv, h_b = _residual_norms(yg.reshape(m, c), a_w_out[0].astype(bf), x2,
                                    kv_norm.reshape(1, d), b_norm[0:1])
    kv = _matmul(h_kv, w_kv.astype(bf), bf, name="matmul_kv")
    proj_b = _matmul(h_b, b_w_in[0].astype(bf), bf, name="matmul_b")
    da = proj_b.shape[1] // 2
    o = _attention(proj_b.reshape(bsz, s, 2 * da), kv.reshape(bsz, s, 2 * da))
    out = _gated_out_norm(o.reshape(m, da), proj_b, b_w_out[0].astype(bf), x1,
                          final_norm.reshape(1, d))
    return out.reshape(bsz, s, d)
```

```python
import functools
import math

import jax
import jax.numpy as jnp
from jax import lax
from jax.experimental import pallas as pl
from jax.experimental.pallas import tpu as pltpu

EPS = 1e-6
LOG2E = 1.0 / math.log(2.0)
LRU_C = 8.0
CONV_W = 4
LRU_BW = 256
N_HEADS = 16
HEAD_DIM = 128
SKIP_BITS = 200.0
DONE_BITS = 1.0e6

SUBLANES = 8
LANES = 128
SEG = SUBLANES
PAD = 4

MIB = 1024 * 1024


def _params(semantics, vmem_mib):
    return pltpu.CompilerParams(dimension_semantics=semantics,
                                vmem_limit_bytes=vmem_mib * MIB)


def _silu(g):
    return g * jax.nn.sigmoid(g)


def _matmul_kernel(a_ref, w_ref, o_ref):
    o_ref[...] = jnp.dot(a_ref[...], w_ref[...],
                         preferred_element_type=jnp.float32).astype(o_ref.dtype)


def _matmul(a, w, out_dtype, *, tm=1024, tn=2048, name):
    m, k = a.shape
    n = w.shape[1]
    return pl.pallas_call(
        _matmul_kernel,
        grid=(m // tm, n // tn),
        in_specs=[
            pl.BlockSpec((tm, k), lambda i, j: (i, 0)),
            pl.BlockSpec((k, tn), lambda i, j: (0, j)),
        ],
        out_specs=pl.BlockSpec((tm, tn), lambda i, j: (i, j)),
        out_shape=jax.ShapeDtypeStruct((m, n), out_dtype),
        compiler_params=_params(("parallel", "parallel"), 40),
        name=name,
    )(a, w)


def _lru_kernel(x_ref, g_ref, wx_ref, wg_ref, cw_ref, cb_ref, wr_ref, br_ref, wi_ref, bi_ref,
                lam_ref, o_ref, h_s, g_s, xext, a_s, u_s, hl_s, al_s, y_s, carry_s):
    t = pl.program_id(2)
    tt = x_ref.shape[0]
    cw = wx_ref.shape[1]
    ns = cw // LANES
    seg = tt // SEG
    pitch = seg + PAD

    x = x_ref[...]
    ms = jnp.mean(x * x, axis=-1, keepdims=True)
    h_s[...] = (x * lax.rsqrt(ms + EPS) * g_ref[...]).astype(h_s.dtype)

    @pl.when(t == 0)
    def _():
        xext[:, 0:SUBLANES, :] = jnp.zeros((ns, SUBLANES, LANES), jnp.float32)
        carry_s[...] = jnp.zeros_like(carry_s)

    @pl.when(t > 0)
    def _():
        xext[:, 0:SUBLANES, :] = xext[:, tt:tt + SUBLANES, :]

    def lanes(s):
        return slice(s * LANES, (s + 1) * LANES)

    nblk = cw // LRU_BW

    def project_x(n):
        xb = jnp.dot(h_s[...], wx_ref[:, n * LRU_BW:(n + 1) * LRU_BW],
                     preferred_element_type=jnp.float32)
        for j, s in enumerate(range(n * LRU_BW // LANES, (n + 1) * LRU_BW // LANES)):
            xext[s, SUBLANES:SUBLANES + tt, :] = xb[:, lanes(j)]

    def project_gate(n):
        g_s[:, n * LRU_BW:(n + 1) * LRU_BW] = jnp.dot(
            h_s[...], wg_ref[:, n * LRU_BW:(n + 1) * LRU_BW],
            preferred_element_type=jnp.float32)

    project_x(0)
    for n in range(nblk):
        slabs = range(n * LRU_BW // LANES, (n + 1) * LRU_BW // LANES)
        if n + 1 < nblk:
            project_x(n + 1)
        else:
            project_gate(0)
        parts = []
        for s in slabs:
            xc = cb_ref[:, lanes(s)]
            for w in range(CONV_W):
                off = SUBLANES - (CONV_W - 1) + w
                xc = xc + cw_ref[w:w + 1, lanes(s)] * xext[s, off:off + tt, :]
            parts.append(xc)
        xc = jnp.concatenate(parts, axis=1)
        sl = slice(n * LRU_BW, (n + 1) * LRU_BW)
        xn = xc.astype(jnp.bfloat16)
        r = jax.nn.sigmoid(
            jnp.dot(xn, wr_ref[n], preferred_element_type=jnp.float32) + br_ref[:, sl])
        i = jax.nn.sigmoid(
            jnp.dot(xn, wi_ref[n], preferred_element_type=jnp.float32) + bi_ref[:, sl])
        neg_lam = -lam_ref[:, sl]
        softplus = jnp.maximum(neg_lam, 0.0) + jnp.log1p(jnp.exp(-jnp.abs(neg_lam)))
        log_a = r * (-LRU_C * softplus)
        a = jnp.exp(log_a)
        one_minus_a2 = -jnp.tanh(log_a) * (a * a + 1.0)
        mult = jnp.where(one_minus_a2 > 0.0, one_minus_a2 * lax.rsqrt(one_minus_a2), 0.0)
        u = mult * (i * xc)
        for j, s in enumerate(slabs):
            for g in range(SEG):
                a_s[s, g * pitch:g * pitch + seg, :] = a[g * seg:(g + 1) * seg, lanes(j)]
                u_s[s, g * pitch:g * pitch + seg, :] = u[g * seg:(g + 1) * seg, lanes(j)]

    for n in range(1, nblk):
        project_gate(n)

    for s in range(ns):
        h = jnp.zeros((SEG, LANES), jnp.float32)
        prod = jnp.ones((SEG, LANES), jnp.float32)
        for p in range(seg):
            ap = a_s[s, pl.ds(p, SEG, stride=pitch), :]
            h = ap * h + u_s[s, pl.ds(p, SEG, stride=pitch), :]
            prod = ap * prod
            hl_s[s, p] = h
            al_s[s, p] = prod
        entering = [carry_s[s]]
        for g in range(SEG - 1):
            entering.append(prod[g:g + 1] * entering[-1] + h[g:g + 1])
        carry_s[s] = prod[SEG - 1:SEG] * entering[-1] + h[SEG - 1:SEG]
        h_in = jnp.concatenate(entering, axis=0)
        for p in range(seg):
            y_s[s, pl.ds(p, SEG, stride=pitch), :] = hl_s[s, p] + al_s[s, p] * h_in

    for s in range(ns):
        for g in range(SEG):
            rows = slice(g * seg, (g + 1) * seg)
            y = y_s[s, g * pitch:g * pitch + seg, :]
            o_ref[rows, lanes(s)] = (y * _silu(g_s[rows, lanes(s)])).astype(o_ref.dtype)


def _lru(x, g, w_in, conv_w, conv_b, w_r, b_r, w_i, b_i, lam, *, tt=256, cw=2560):
    bsz, s, d = x.shape
    c = w_in.shape[1] // 2
    nc = c // cw
    nb = cw // LRU_BW
    ns = cw // LANES
    seg = tt // SEG
    row = lambda b, ci, t: (0, ci)
    return pl.pallas_call(
        _lru_kernel,
        grid=(bsz, nc, s // tt),
        in_specs=[
            pl.BlockSpec((None, tt, d), lambda b, ci, t: (b, t, 0)),
            pl.BlockSpec((1, d), lambda b, ci, t: (0, 0)),
            pl.BlockSpec((d, cw), lambda b, ci, t: (0, ci), pipeline_mode=pl.Buffered(1)),
            pl.BlockSpec((d, cw), lambda b, ci, t: (0, nc + ci), pipeline_mode=pl.Buffered(1)),
            pl.BlockSpec((CONV_W, cw), row),
            pl.BlockSpec((1, cw), row),
            pl.BlockSpec((nb, LRU_BW, LRU_BW), lambda b, ci, t: (ci, 0, 0)),
            pl.BlockSpec((1, cw), row),
            pl.BlockSpec((nb, LRU_BW, LRU_BW), lambda b, ci, t: (ci, 0, 0)),
            pl.BlockSpec((1, cw), row),
            pl.BlockSpec((1, cw), row),
        ],
        out_specs=pl.BlockSpec((None, tt, cw), lambda b, ci, t: (b, t, ci)),
        out_shape=jax.ShapeDtypeStruct((bsz, s, c), jnp.bfloat16),
        scratch_shapes=[
            pltpu.VMEM((tt, d), jnp.bfloat16),
            pltpu.VMEM((tt, cw), jnp.float32),
            pltpu.VMEM((ns, tt + SUBLANES, LANES), jnp.float32),
            pltpu.VMEM((ns, SEG * (seg + PAD), LANES), jnp.float32),
            pltpu.VMEM((ns, SEG * (seg + PAD), LANES), jnp.float32),
            pltpu.VMEM((ns, seg, SEG, LANES), jnp.float32),
            pltpu.VMEM((ns, seg, SEG, LANES), jnp.float32),
            pltpu.VMEM((ns, SEG * (seg + PAD), LANES), jnp.float32),
            pltpu.VMEM((ns, 1, LANES), jnp.float32),
        ],
        compiler_params=_params(("parallel", "parallel", "arbitrary"), 52),
        name="lru",
    )(x, g, w_in, w_in, conv_w, conv_b, w_r, b_r, w_i, b_i, lam)


def _residual_norms_kernel(a_ref, w_ref, r_ref, g1_ref, g2_ref, x1_ref, h1_ref, h2_ref):
    x1 = r_ref[...] + jnp.dot(a_ref[...], w_ref[...], preferred_element_type=jnp.float32)
    x1_ref[...] = x1
    xn = x1 * lax.rsqrt(jnp.mean(x1 * x1, axis=-1, keepdims=True) + EPS)
    h1_ref[...] = (xn * g1_ref[...]).astype(h1_ref.dtype)
    h2_ref[...] = (xn * g2_ref[...]).astype(h2_ref.dtype)


def _residual_norms(a, w, res, g1, g2, *, tm=512):
    m, k = a.shape
    d = w.shape[1]
    rows = pl.BlockSpec((tm, d), lambda i: (i, 0))
    gain = pl.BlockSpec((1, d), lambda i: (0, 0))
    return pl.pallas_call(
        _residual_norms_kernel,
        grid=(m // tm,),
        in_specs=[
            pl.BlockSpec((tm, k), lambda i: (i, 0)),
            pl.BlockSpec((k, d), lambda i: (0, 0), pipeline_mode=pl.Buffered(1)),
            rows, gain, gain,
        ],
        out_specs=[rows, rows, rows],
        out_shape=[jax.ShapeDtypeStruct((m, d), jnp.float32),
                   jax.ShapeDtypeStruct((m, d), jnp.bfloat16),
                   jax.ShapeDtypeStruct((m, d), jnp.bfloat16)],
        compiler_params=_params(("parallel",), 48),
        name="residual_norms",
    )(a, w, res, g1, g2)


def _attention_kernel(q_ref, k_ref, v_ref, o_ref, q_s, z_s, w_s, c_ref, acc_ref, *, tk):
    i = pl.program_id(2)
    tq, dh = q_ref.shape
    nsub = tq // tk
    q_s[...] = (q_ref[...].astype(jnp.float32) * (LOG2E / math.sqrt(dh))).astype(jnp.bfloat16)

    rows = lax.broadcasted_iota(jnp.int32, (tk, tk), 0)
    cols = lax.broadcasted_iota(jnp.int32, (tk, tk), 1)
    tri = (rows >= cols).astype(jnp.bfloat16)

    c_ref[...] = jnp.zeros_like(c_ref)
    acc_ref[...] = jnp.zeros_like(acc_ref)

    def sub(a):
        return pl.ds(a * tk, tk)

    def key_block(d, a):
        blk = i * nsub + a - d
        return pl.ds(pl.multiple_of(jnp.maximum(blk, 0) * tk, tk), tk), blk >= 0

    def scores(q, kb):
        return lax.dot_general(q, kb, (((1,), (1,)), ((), ())),
                               preferred_element_type=jnp.float32)

    def weights(z, c, mask):
        sp = jnp.maximum(z, 0.0) + jnp.log(1.0 + jnp.exp2(-jnp.abs(z))) * LOG2E
        if mask is not None:
            sp = jnp.where(mask, sp, 0.0)
        incl = jnp.dot(sp.astype(jnp.bfloat16), tri, preferred_element_type=jnp.float32)
        w = jnp.exp2(z - incl - c)
        if mask is not None:
            w = jnp.where(mask, w, 0.0)
        return w.astype(jnp.bfloat16), incl[:, 0:1]

    def stage_scores(d, par):
        for a in range(nsub):
            keys, _ = key_block(d, a)
            z_s[par, sub(a), :] = scores(q_s[sub(a), :], k_ref[keys, :])

    def stage_weights(d, par, diagonal=False):
        if diagonal:
            row = lax.broadcasted_iota(jnp.int32, (tq, tk), 0) & (tk - 1)
            mask = lax.broadcasted_iota(jnp.int32, (tq, tk), 1) < row
        else:
            mask = None
            for a in range(nsub):
                _, exists = key_block(d, a)
                c_ref[sub(a), :] += jnp.where(exists, 0.0, DONE_BITS)
        w, total = weights(z_s[par], c_ref[...], mask)
        w_s[par] = w
        c_ref[...] += total

    def stage_values(d, par):
        for a in range(nsub):
            keys, _ = key_block(d, a)
            acc_ref[sub(a), :] += jnp.dot(w_s[par, sub(a), :], v_ref[keys, :],
                                          preferred_element_type=jnp.float32)

    stage_scores(0, 1)
    stage_scores(1, 0)
    stage_weights(0, 1, diagonal=True)
    stage_values(0, 1)
    stage_weights(1, 0)
    c_min1 = jnp.min(c_ref[...])

    @pl.when(c_min1 < SKIP_BITS)
    def _():
        stage_scores(2, 1)

    def body(carry):
        m, _ = carry
        d = 2 * m + 2
        stage_scores(d + 1, 0)
        stage_values(d - 1, 0)
        stage_weights(d, 1)
        c_min = jnp.min(c_ref[...])
        stage_scores(d + 2, 1)
        stage_values(d, 1)
        stage_weights(d + 1, 0)
        return m + 1, c_min

    def cond(carry):
        m, c_min = carry
        return jnp.logical_and(m < n_pairs, c_min < SKIP_BITS)

    n_pairs = (i * nsub + nsub - 2) // 2
    m_done, _ = lax.while_loop(cond, body, (jnp.int32(0), c_min1))
    stage_values(2 * m_done + 1, 0)
    o_ref[...] = acc_ref[...].astype(o_ref.dtype)


def _attention(proj_b, kv, *, tq=4096, tk=256):
    bsz, s, _ = proj_b.shape
    kern = functools.partial(_attention_kernel, tk=tk)
    return pl.pallas_call(
        kern,
        grid=(bsz, N_HEADS, s // tq),
        in_specs=[
            pl.BlockSpec((None, tq, HEAD_DIM), lambda b, h, i: (b, i, h)),
            pl.BlockSpec((None, s, HEAD_DIM), lambda b, h, i: (b, 0, h)),
            pl.BlockSpec((None, s, HEAD_DIM), lambda b, h, i: (b, 0, N_HEADS + h)),
        ],
        out_specs=pl.BlockSpec((None, tq, HEAD_DIM), lambda b, h, i: (b, i, h)),
        out_shape=jax.ShapeDtypeStruct((bsz, s, N_HEADS * HEAD_DIM), jnp.bfloat16),
        scratch_shapes=[
            pltpu.VMEM((tq, HEAD_DIM), jnp.bfloat16),
            pltpu.VMEM((2, tq, tk), jnp.float32),
            pltpu.VMEM((2, tq, tk), jnp.bfloat16),
            pltpu.VMEM((tq, 1), jnp.float32),
            pltpu.VMEM((tq, HEAD_DIM), jnp.float32),
        ],
        compiler_params=_params(("parallel", "parallel", "arbitrary"), 40),
        name="attention",
    )(proj_b, kv, kv)


def _gated_out_norm_kernel(o_ref, g_ref, w_ref, x_ref, fn_ref, out_ref):
    a = (o_ref[...].astype(jnp.float32) * _silu(g_ref[...].astype(jnp.float32))).astype(jnp.bfloat16)
    y = x_ref[...] + jnp.dot(a, w_ref[...], preferred_element_type=jnp.float32)
    ms = jnp.mean(y * y, axis=-1, keepdims=True)
    out_ref[...] = y * lax.rsqrt(ms + EPS) * fn_ref[...]


def _gated_out_norm(o, proj_b, w, x1, fn, *, tm=512):
    m, d = x1.shape
    return pl.pallas_call(
        _gated_out_norm_kernel,
        grid=(m // tm,),
        in_specs=[
            pl.BlockSpec((tm, d), lambda i: (i, 0)),
            pl.BlockSpec((tm, d), lambda i: (i, 1)),
            pl.BlockSpec((d, d), lambda i: (0, 0), pipeline_mode=pl.Buffered(1)),
            pl.BlockSpec((tm, d), lambda i: (i, 0)),
            pl.BlockSpec((1, d), lambda i: (0, 0)),
        ],
        out_specs=pl.BlockSpec((tm, d), lambda i: (i, 0)),
        out_shape=jax.ShapeDtypeStruct((m, d), jnp.float32),
        compiler_params=_params(("parallel",), 48),
        name="gated_out_norm",
    )(o, proj_b, w, x1, fn)


def kernel(x, a_norm, a_w_in, a_conv_w, a_conv_b, a_w_r, a_b_r, a_w_i, a_b_i, a_lambda,
           a_w_out, kv_norm, w_kv, b_norm, b_w_in, b_w_out, final_norm):
    bsz, s, d = x.shape
    m = bsz * s
    bf = jnp.bfloat16
    x2 = x.reshape(m, d)

    c = a_w_in.shape[2] // 2
    yg = _lru(x, a_norm[0:1], a_w_in[0].astype(bf), a_conv_w[0], a_conv_b[0:1],
              a_w_r[0].astype(bf), a_b_r[0:1], a_w_i[0].astype(bf), a_b_i[0:1],
              a_lambda[0:1])
    x1, h_kv, h_b = _residual_norms(yg.reshape(m, c), a_w_out[0].astype(bf), x2,
                                    kv_norm.reshape(1, d), b_norm[0:1])
    kv = _matmul(h_kv, w_kv.astype(bf), bf, name="matmul_kv")
    proj_b = _matmul(h_b, b_w_in[0].astype(bf), bf, name="matmul_b")
    da = proj_b.shape[1] // 2
    o = _attention(proj_b.reshape(bsz, s, 2 * da), kv.reshape(bsz, s, 2 * da))
    out = _gated_out_norm(o.reshape(m, da), proj_b, b_w_out[0].astype(bf), x1,
                          final_norm.reshape(1, d))
    return out.reshape(bsz, s, d)
```

```python
import functools
import math

import jax
import jax.numpy as jnp
from jax import lax
from jax.experimental import pallas as pl
from jax.experimental.pallas import tpu as pltpu

EPS = 1e-6
LOG2E = 1.0 / math.log(2.0)
LRU_C = 8.0
CONV_W = 4
LRU_BW = 256
N_HEADS = 16
HEAD_DIM = 128
SKIP_BITS = 200.0
DONE_BITS = 1.0e6

SUBLANES = 8
LANES = 128
SEG = SUBLANES
PAD = 4

MIB = 1024 * 1024


def _params(semantics, vmem_mib):
    return pltpu.CompilerParams(dimension_semantics=semantics,
                                vmem_limit_bytes=vmem_mib * MIB)


def _silu(g):
    return g * jax.nn.sigmoid(g)


def _matmul_kernel(a_ref, w_ref, o_ref):
    o_ref[...] = jnp.dot(a_ref[...], w_ref[...],
                         preferred_element_type=jnp.float32).astype(o_ref.dtype)


def _matmul(a, w, out_dtype, *, tm=1024, tn=2048, name):
    m, k = a.shape
    n = w.shape[1]
    return pl.pallas_call(
        _matmul_kernel,
        grid=(m // tm, n // tn),
        in_specs=[
            pl.BlockSpec((tm, k), lambda i, j: (i, 0)),
            pl.BlockSpec((k, tn), lambda i, j: (0, j)),
        ],
        out_specs=pl.BlockSpec((tm, tn), lambda i, j: (i, j)),
        out_shape=jax.ShapeDtypeStruct((m, n), out_dtype),
        compiler_params=_params(("parallel", "parallel"), 40),
        name=name,
    )(a, w)


def _lru_kernel(x_ref, g_ref, wx_ref, wg_ref, cw_ref, cb_ref, wr_ref, br_ref, wi_ref, bi_ref,
                lam_ref, o_ref, h_s, g_s, xext, a_s, u_s, hl_s, al_s, y_s, carry_s):
    t = pl.program_id(2)
    tt = x_ref.shape[0]
    cw = wx_ref.shape[1]
    ns = cw // LANES
    seg = tt // SEG
    pitch = seg + PAD

    x = x_ref[...]
    ms = jnp.mean(x * x, axis=-1, keepdims=True)
    h_s[...] = (x * lax.rsqrt(ms + EPS) * g_ref[...]).astype(h_s.dtype)

    @pl.when(t == 0)
    def _():
        xext[:, 0:SUBLANES, :] = jnp.zeros((ns, SUBLANES, LANES), jnp.float32)
        carry_s[...] = jnp.zeros_like(carry_s)

    @pl.when(t > 0)
    def _():
        xext[:, 0:SUBLANES, :] = xext[:, tt:tt + SUBLANES, :]

    def lanes(s):
        return slice(s * LANES, (s + 1) * LANES)

    nblk = cw // LRU_BW

    def project_x(n):
        xb = jnp.dot(h_s[...], wx_ref[:, n * LRU_BW:(n + 1) * LRU_BW],
                     preferred_element_type=jnp.float32)
        for j, s in enumerate(range(n * LRU_BW // LANES, (n + 1) * LRU_BW // LANES)):
            xext[s, SUBLANES:SUBLANES + tt, :] = xb[:, lanes(j)]

    def project_gate(n):
        g_s[:, n * LRU_BW:(n + 1) * LRU_BW] = jnp.dot(
            h_s[...], wg_ref[:, n * LRU_BW:(n + 1) * LRU_BW],
            preferred_element_type=jnp.float32)

    project_x(0)
    for n in range(nblk):
        slabs = range(n * LRU_BW // LANES, (n + 1) * LRU_BW // LANES)
        if n + 1 < nblk:
            project_x(n + 1)
        else:
            project_gate(0)
        parts = []
        for s in slabs:
            xc = cb_ref[:, lanes(s)]
            for w in range(CONV_W):
                off = SUBLANES - (CONV_W - 1) + w
                xc = xc + cw_ref[w:w + 1, lanes(s)] * xext[s, off:off + tt, :]
            parts.append(xc)
        xc = jnp.concatenate(parts, axis=1)
        sl = slice(n * LRU_BW, (n + 1) * LRU_BW)
        xn = xc.astype(jnp.bfloat16)
        r = jax.nn.sigmoid(
            jnp.dot(xn, wr_ref[n], preferred_element_type=jnp.float32) + br_ref[:, sl])
        i = jax.nn.sigmoid(
            jnp.dot(xn, wi_ref[n], preferred_element_type=jnp.float32) + bi_ref[:, sl])
        neg_lam = -lam_ref[:, sl]
        softplus = jnp.maximum(neg_lam, 0.0) + jnp.log1p(jnp.exp(-jnp.abs(neg_lam)))
        log_a = r * (-LRU_C * softplus)
        a = jnp.exp(log_a)
        one_minus_a2 = -jnp.tanh(log_a) * (a * a + 1.0)
        mult = jnp.where(one_minus_a2 > 0.0, one_minus_a2 * lax.rsqrt(one_minus_a2), 0.0)
        u = mult * (i * xc)
        for j, s in enumerate(slabs):
            for g in range(SEG):
                a_s[s, g * pitch:g * pitch + seg, :] = a[g * seg:(g + 1) * seg, lanes(j)]
                u_s[s, g * pitch:g * pitch + seg, :] = u[g * seg:(g + 1) * seg, lanes(j)]

    for n in range(1, nblk):
        project_gate(n)

    for s in range(ns):
        h = jnp.zeros((SEG, LANES), jnp.float32)
        prod = jnp.ones((SEG, LANES), jnp.float32)
        for p in range(seg):
            ap = a_s[s, pl.ds(p, SEG, stride=pitch), :]
            h = ap * h + u_s[s, pl.ds(p, SEG, stride=pitch), :]
            prod = ap * prod
            hl_s[s, p] = h
            al_s[s, p] = prod
        entering = [carry_s[s]]
        for g in range(SEG - 1):
            entering.append(prod[g:g + 1] * entering[-1] + h[g:g + 1])
        carry_s[s] = prod[SEG - 1:SEG] * entering[-1] + h[SEG - 1:SEG]
        h_in = jnp.concatenate(entering, axis=0)
        for p in range(seg):
            y_s[s, pl.ds(p, SEG, stride=pitch), :] = hl_s[s, p] + al_s[s, p] * h_in

    for s in range(ns):
        for g in range(SEG):
            rows = slice(g * seg, (g + 1) * seg)
            y = y_s[s, g * pitch:g * pitch + seg, :]
            o_ref[rows, lanes(s)] = (y * _silu(g_s[rows, lanes(s)])).astype(o_ref.dtype)


def _lru(x, g, w_in, conv_w, conv_b, w_r, b_r, w_i, b_i, lam, *, tt=256, cw=2560):
    bsz, s, d = x.shape
    c = w_in.shape[1] // 2
    nc = c // cw
    nb = cw // LRU_BW
    ns = cw // LANES
    seg = tt // SEG
    row = lambda b, ci, t: (0, ci)
    return pl.pallas_call(
        _lru_kernel,
        grid=(bsz, nc, s // tt),
        in_specs=[
            pl.BlockSpec((None, tt, d), lambda b, ci, t: (b, t, 0)),
            pl.BlockSpec((1, d), lambda b, ci, t: (0, 0)),
            pl.BlockSpec((d, cw), lambda b, ci, t: (0, ci), pipeline_mode=pl.Buffered(1)),
            pl.BlockSpec((d, cw), lambda b, ci, t: (0, nc + ci), pipeline_mode=pl.Buffered(1)),
            pl.BlockSpec((CONV_W, cw), row),
            pl.BlockSpec((1, cw), row),
            pl.BlockSpec((nb, LRU_BW, LRU_BW), lambda b, ci, t: (ci, 0, 0)),
            pl.BlockSpec((1, cw), row),
            pl.BlockSpec((nb, LRU_BW, LRU_BW), lambda b, ci, t: (ci, 0, 0)),
            pl.BlockSpec((1, cw), row),
            pl.BlockSpec((1, cw), row),
        ],
        out_specs=pl.BlockSpec((None, tt, cw), lambda b, ci, t: (b, t, ci)),
        out_shape=jax.ShapeDtypeStruct((bsz, s, c), jnp.bfloat16),
        scratch_shapes=[
            pltpu.VMEM((tt, d), jnp.bfloat16),
            pltpu.VMEM((tt, cw), jnp.float32),
            pltpu.VMEM((ns, tt + SUBLANES, LANES), jnp.float32),
            pltpu.VMEM((ns, SEG * (seg + PAD), LANES), jnp.float32),
            pltpu.VMEM((ns, SEG * (seg + PAD), LANES), jnp.float32),
            pltpu.VMEM((ns, seg, SEG, LANES), jnp.float32),
            pltpu.VMEM((ns, seg, SEG, LANES), jnp.float32),
            pltpu.VMEM((ns, SEG * (seg + PAD), LANES), jnp.float32),
            pltpu.VMEM((ns, 1, LANES), jnp.float32),
        ],
        compiler_params=_params(("parallel", "parallel", "arbitrary"), 52),
        name="lru",
    )(x, g, w_in, w_in, conv_w, conv_b, w_r, b_r, w_i, b_i, lam)


def _residual_norm_kernel(a_ref, w_ref, r_ref, x1_ref, xn_ref):
    x1 = r_ref[...] + jnp.dot(a_ref[...], w_ref[...], preferred_element_type=jnp.float32)
    x1_ref[...] = x1
    ms = jnp.mean(x1 * x1, axis=-1, keepdims=True)
    xn_ref[...] = (x1 * lax.rsqrt(ms + EPS)).astype(xn_ref.dtype)


def _residual_norm(a, w, res, *, tm=512):
    m, k = a.shape
    d = w.shape[1]
    rows = pl.BlockSpec((tm, d), lambda i: (i, 0))
    return pl.pallas_call(
        _residual_norm_kernel,
        grid=(m // tm,),
        in_specs=[
            pl.BlockSpec((tm, k), lambda i: (i, 0)),
            pl.BlockSpec((k, d), lambda i: (0, 0), pipeline_mode=pl.Buffered(1)),
            rows,
        ],
        out_specs=[rows, rows],
        out_shape=[jax.ShapeDtypeStruct((m, d), jnp.float32),
                   jax.ShapeDtypeStruct((m, d), jnp.bfloat16)],
        compiler_params=_params(("parallel",), 48),
        name="residual_norm",
    )(a, w, res)


def _attention_kernel(q_ref, k_ref, v_ref, gate_ref, o_ref, q_s, z_s, w_s, c_ref, acc_ref, *, tk):
    i = pl.program_id(2)
    tq, dh = q_ref.shape
    nsub = tq // tk
    q_s[...] = (q_ref[...].astype(jnp.float32) * (LOG2E / math.sqrt(dh))).astype(jnp.bfloat16)

    rows = lax.broadcasted_iota(jnp.int32, (tk, tk), 0)
    cols = lax.broadcasted_iota(jnp.int32, (tk, tk), 1)
    tri = (rows >= cols).astype(jnp.bfloat16)

    c_ref[...] = jnp.zeros_like(c_ref)
    acc_ref[...] = jnp.zeros_like(acc_ref)

    def sub(a):
        return pl.ds(a * tk, tk)

    def key_block(d, a):
        blk = i * nsub + a - d
        return pl.ds(pl.multiple_of(jnp.maximum(blk, 0) * tk, tk), tk), blk >= 0

    def scores(q, kb):
        return lax.dot_general(q, kb, (((1,), (1,)), ((), ())),
                               preferred_element_type=jnp.float32)

    def weights(z, c, mask):
        sp = jnp.maximum(z, 0.0) + jnp.log(1.0 + jnp.exp2(-jnp.abs(z))) * LOG2E
        if mask is not None:
            sp = jnp.where(mask, sp, 0.0)
        incl = jnp.dot(sp.astype(jnp.bfloat16), tri, preferred_element_type=jnp.float32)
        w = jnp.exp2(z - incl - c)
        if mask is not None:
            w = jnp.where(mask, w, 0.0)
        return w.astype(jnp.bfloat16), incl[:, 0:1]

    def stage_scores(d, par):
        for a in range(nsub):
            keys, _ = key_block(d, a)
            z_s[par, sub(a), :] = scores(q_s[sub(a), :], k_ref[keys, :])

    def stage_weights(d, par, diagonal=False):
        if diagonal:
            row = lax.broadcasted_iota(jnp.int32, (tq, tk), 0) & (tk - 1)
            mask = lax.broadcasted_iota(jnp.int32, (tq, tk), 1) < row
        else:
            mask = None
            for a in range(nsub):
                _, exists = key_block(d, a)
                c_ref[sub(a), :] += jnp.where(exists, 0.0, DONE_BITS)
        w, total = weights(z_s[par], c_ref[...], mask)
        w_s[par] = w
        c_ref[...] += total

    def stage_values(d, par):
        for a in range(nsub):
            keys, _ = key_block(d, a)
            acc_ref[sub(a), :] += jnp.dot(w_s[par, sub(a), :], v_ref[keys, :],
                                          preferred_element_type=jnp.float32)

    stage_scores(0, 1)
    stage_scores(1, 0)
    stage_weights(0, 1, diagonal=True)
    stage_values(0, 1)
    stage_weights(1, 0)
    c_min1 = jnp.min(c_ref[...])

    @pl.when(c_min1 < SKIP_BITS)
    def _():
        stage_scores(2, 1)

    def body(carry):
        m, _ = carry
        d = 2 * m + 2
        stage_scores(d + 1, 0)
        stage_values(d - 1, 0)
        stage_weights(d, 1)
        c_min = jnp.min(c_ref[...])
        stage_scores(d + 2, 1)
        stage_values(d, 1)
        stage_weights(d + 1, 0)
        return m + 1, c_min

    def cond(carry):
        m, c_min = carry
        return jnp.logical_and(m < n_pairs, c_min < SKIP_BITS)

    n_pairs = (i * nsub + nsub - 2) // 2
    m_done, _ = lax.while_loop(cond, body, (jnp.int32(0), c_min1))
    stage_values(2 * m_done + 1, 0)
    o_ref[...] = (acc_ref[...] * _silu(gate_ref[...].astype(jnp.float32))).astype(o_ref.dtype)


def _attention(proj, *, tq=4096, tk=256):
    bsz, s, _ = proj.shape
    tile = lambda part: pl.BlockSpec((None, tq, HEAD_DIM),
                                     lambda b, h, i: (b, i, part * N_HEADS + h))
    keys = lambda part: pl.BlockSpec((None, s, HEAD_DIM),
                                     lambda b, h, i: (b, 0, part * N_HEADS + h))
    return pl.pallas_call(
        functools.partial(_attention_kernel, tk=tk),
        grid=(bsz, N_HEADS, s // tq),
        in_specs=[tile(2), keys(0), keys(1), tile(3)],
        out_specs=pl.BlockSpec((None, tq, HEAD_DIM), lambda b, h, i: (b, i, h)),
        out_shape=jax.ShapeDtypeStruct((bsz, s, N_HEADS * HEAD_DIM), jnp.bfloat16),
        scratch_shapes=[
            pltpu.VMEM((tq, HEAD_DIM), jnp.bfloat16),
            pltpu.VMEM((2, tq, tk), jnp.float32),
            pltpu.VMEM((2, tq, tk), jnp.bfloat16),
            pltpu.VMEM((tq, 1), jnp.float32),
            pltpu.VMEM((tq, HEAD_DIM), jnp.float32),
        ],
        compiler_params=_params(("parallel", "parallel", "arbitrary"), 44),
        name="attention",
    )(proj, proj, proj, proj)


def _out_norm_kernel(a_ref, w_ref, x_ref, fn_ref, out_ref):
    y = x_ref[...] + jnp.dot(a_ref[...], w_ref[...], preferred_element_type=jnp.float32)
    ms = jnp.mean(y * y, axis=-1, keepdims=True)
    out_ref[...] = y * lax.rsqrt(ms + EPS) * fn_ref[...]


def _out_norm(a, w, x1, fn, *, tm=512):
    m, d = x1.shape
    rows = pl.BlockSpec((tm, d), lambda i: (i, 0))
    return pl.pallas_call(
        _out_norm_kernel,
        grid=(m // tm,),
        in_specs=[
            pl.BlockSpec((tm, a.shape[1]), lambda i: (i, 0)),
            pl.BlockSpec(w.shape, lambda i: (0, 0), pipeline_mode=pl.Buffered(1)),
            rows,
            pl.BlockSpec((1, d), lambda i: (0, 0)),
        ],
        out_specs=rows,
        out_shape=jax.ShapeDtypeStruct((m, d), jnp.float32),
        compiler_params=_params(("parallel",), 48),
        name="out_norm",
    )(a, w, x1, fn)


def kernel(x, a_norm, a_w_in, a_conv_w, a_conv_b, a_w_r, a_b_r, a_w_i, a_b_i, a_lambda,
           a_w_out, kv_norm, w_kv, b_norm, b_w_in, b_w_out, final_norm):
    bsz, s, d = x.shape
    m = bsz * s
    bf = jnp.bfloat16
    x2 = x.reshape(m, d)

    c = a_w_in.shape[2] // 2
    yg = _lru(x, a_norm[0:1], a_w_in[0].astype(bf), a_conv_w[0], a_conv_b[0:1],
              a_w_r[0].astype(bf), a_b_r[0:1], a_w_i[0].astype(bf), a_b_i[0:1],
              a_lambda[0:1])
    x1, xn = _residual_norm(yg.reshape(m, c), a_w_out[0].astype(bf), x2)
    w_proj = jnp.concatenate([kv_norm[:, None] * w_kv, b_norm[0][:, None] * b_w_in[0]],
                             axis=1).astype(bf)
    proj = _matmul(xn, w_proj, bf, name="matmul_kvqg")
    og = _attention(proj.reshape(bsz, s, w_proj.shape[1]))
    out = _out_norm(og.reshape(m, -1), b_w_out[0].astype(bf), x1, final_norm.reshape(1, d))
    return out.reshape(bsz, s, d)
```

```python
import functools
import math

import jax
import jax.numpy as jnp
from jax import lax
from jax.experimental import pallas as pl
from jax.experimental.pallas import tpu as pltpu

EPS = 1e-6
LOG2E = 1.0 / math.log(2.0)
LRU_C = 8.0
CONV_W = 4
LRU_BW = 256
N_HEADS = 16
HEAD_DIM = 128
SKIP_BITS = 200.0
DONE_BITS = 1.0e6

SUBLANES = 8
LANES = 128
SEG = SUBLANES
PAD = 4

MIB = 1024 * 1024


def _params(semantics, vmem_mib):
    return pltpu.CompilerParams(dimension_semantics=semantics,
                                vmem_limit_bytes=vmem_mib * MIB)


def _silu(g):
    return g * jax.nn.sigmoid(g)


def _matmul_kernel(a_ref, w_ref, o_ref):
    o_ref[...] = jnp.dot(a_ref[...], w_ref[...],
                         preferred_element_type=jnp.float32).astype(o_ref.dtype)


def _matmul(a, w, out_dtype, *, tm=1024, tn=2048, name):
    m, k = a.shape
    n = w.shape[1]
    return pl.pallas_call(
        _matmul_kernel,
        grid=(m // tm, n // tn),
        in_specs=[
            pl.BlockSpec((tm, k), lambda i, j: (i, 0)),
            pl.BlockSpec((k, tn), lambda i, j: (0, j)),
        ],
        out_specs=pl.BlockSpec((tm, tn), lambda i, j: (i, j)),
        out_shape=jax.ShapeDtypeStruct((m, n), out_dtype),
        compiler_params=_params(("parallel", "parallel"), 40),
        name=name,
    )(a, w)


def _lru_kernel(x_ref, g_ref, wx_ref, wg_ref, cw_ref, cb_ref, wr_ref, br_ref, wi_ref, bi_ref,
                lam_ref, o_ref, h_s, g_s, xext, a_s, u_s, hl_s, al_s, y_s, carry_s):
    t = pl.program_id(2)
    tt = x_ref.shape[0]
    cw = wx_ref.shape[1]
    ns = cw // LANES
    seg = tt // SEG
    pitch = seg + PAD

    x = x_ref[...]
    ms = jnp.mean(x * x, axis=-1, keepdims=True)
    h_s[...] = (x * lax.rsqrt(ms + EPS) * g_ref[...]).astype(h_s.dtype)

    @pl.when(t == 0)
    def _():
        xext[:, 0:SUBLANES, :] = jnp.zeros((ns, SUBLANES, LANES), jnp.float32)
        carry_s[...] = jnp.zeros_like(carry_s)

    @pl.when(t > 0)
    def _():
        xext[:, 0:SUBLANES, :] = xext[:, tt:tt + SUBLANES, :]

    def lanes(s):
        return slice(s * LANES, (s + 1) * LANES)

    nblk = cw // LRU_BW

    def project_x(n):
        xb = jnp.dot(h_s[...], wx_ref[:, n * LRU_BW:(n + 1) * LRU_BW],
                     preferred_element_type=jnp.float32)
        for j, s in enumerate(range(n * LRU_BW // LANES, (n + 1) * LRU_BW // LANES)):
            xext[s, SUBLANES:SUBLANES + tt, :] = xb[:, lanes(j)]

    def project_gate(n):
        g_s[:, n * LRU_BW:(n + 1) * LRU_BW] = jnp.dot(
            h_s[...], wg_ref[:, n * LRU_BW:(n + 1) * LRU_BW],
            preferred_element_type=jnp.float32)

    project_x(0)
    for n in range(nblk):
        slabs = range(n * LRU_BW // LANES, (n + 1) * LRU_BW // LANES)
        if n + 1 < nblk:
            project_x(n + 1)
        else:
            project_gate(0)
        parts = []
        for s in slabs:
            xc = cb_ref[:, lanes(s)]
            for w in range(CONV_W):
                off = SUBLANES - (CONV_W - 1) + w
                xc = xc + cw_ref[w:w + 1, lanes(s)] * xext[s, off:off + tt, :]
            parts.append(xc)
        xc = jnp.concatenate(parts, axis=1)
        sl = slice(n * LRU_BW, (n + 1) * LRU_BW)
        xn = xc.astype(jnp.bfloat16)
        r = jax.nn.sigmoid(
            jnp.dot(xn, wr_ref[n], preferred_element_type=jnp.float32) + br_ref[:, sl])
        i = jax.nn.sigmoid(
            jnp.dot(xn, wi_ref[n], preferred_element_type=jnp.float32) + bi_ref[:, sl])
        neg_lam = -lam_ref[:, sl]
        softplus = jnp.maximum(neg_lam, 0.0) + jnp.log1p(jnp.exp(-jnp.abs(neg_lam)))
        log_a = r * (-LRU_C * softplus)
        a = jnp.exp(log_a)
        one_minus_a2 = -jnp.tanh(log_a) * (a * a + 1.0)
        mult = jnp.where(one_minus_a2 > 0.0, one_minus_a2 * lax.rsqrt(one_minus_a2), 0.0)
        u = mult * (i * xc)
        for j, s in enumerate(slabs):
            for g in range(SEG):
                a_s[s, g * pitch:g * pitch + seg, :] = a[g * seg:(g + 1) * seg, lanes(j)]
                u_s[s, g * pitch:g * pitch + seg, :] = u[g * seg:(g + 1) * seg, lanes(j)]

    for n in range(1, nblk):
        project_gate(n)

    for s in range(ns):
        h = jnp.zeros((SEG, LANES), jnp.float32)
        prod = jnp.ones((SEG, LANES), jnp.float32)
        for p in range(seg):
            ap = a_s[s, pl.ds(p, SEG, stride=pitch), :]
            h = ap * h + u_s[s, pl.ds(p, SEG, stride=pitch), :]
            prod = ap * prod
            hl_s[s, p] = h
            al_s[s, p] = prod
        entering = [carry_s[s]]
        for g in range(SEG - 1):
            entering.append(prod[g:g + 1] * entering[-1] + h[g:g + 1])
        carry_s[s] = prod[SEG - 1:SEG] * entering[-1] + h[SEG - 1:SEG]
        h_in = jnp.concatenate(entering, axis=0)
        for p in range(seg):
            y_s[s, pl.ds(p, SEG, stride=pitch), :] = hl_s[s, p] + al_s[s, p] * h_in

    for s in range(ns):
        for g in range(SEG):
            rows = slice(g * seg, (g + 1) * seg)
            y = y_s[s, g * pitch:g * pitch + seg, :]
            o_ref[rows, lanes(s)] = (y * _silu(g_s[rows, lanes(s)])).astype(o_ref.dtype)


def _lru(x, g, w_in, conv_w, conv_b, w_r, b_r, w_i, b_i, lam, *, tt=256, cw=2560):
    bsz, s, d = x.shape
    c = w_in.shape[1] // 2
    nc = c // cw
    nb = cw // LRU_BW
    ns = cw // LANES
    seg = tt // SEG
    row = lambda b, ci, t: (0, ci)
    return pl.pallas_call(
        _lru_kernel,
        grid=(bsz, nc, s // tt),
        in_specs=[
            pl.BlockSpec((None, tt, d), lambda b, ci, t: (b, t, 0)),
            pl.BlockSpec((1, d), lambda b, ci, t: (0, 0)),
            pl.BlockSpec((d, cw), lambda b, ci, t: (0, ci), pipeline_mode=pl.Buffered(1)),
            pl.BlockSpec((d, cw), lambda b, ci, t: (0, nc + ci), pipeline_mode=pl.Buffered(1)),
            pl.BlockSpec((CONV_W, cw), row),
            pl.BlockSpec((1, cw), row),
            pl.BlockSpec((nb, LRU_BW, LRU_BW), lambda b, ci, t: (ci, 0, 0)),
            pl.BlockSpec((1, cw), row),
            pl.BlockSpec((nb, LRU_BW, LRU_BW), lambda b, ci, t: (ci, 0, 0)),
            pl.BlockSpec((1, cw), row),
            pl.BlockSpec((1, cw), row),
        ],
        out_specs=pl.BlockSpec((None, tt, cw), lambda b, ci, t: (b, t, ci)),
        out_shape=jax.ShapeDtypeStruct((bsz, s, c), jnp.bfloat16),
        scratch_shapes=[
            pltpu.VMEM((tt, d), jnp.bfloat16),
            pltpu.VMEM((tt, cw), jnp.float32),
            pltpu.VMEM((ns, tt + SUBLANES, LANES), jnp.float32),
            pltpu.VMEM((ns, SEG * (seg + PAD), LANES), jnp.float32),
            pltpu.VMEM((ns, SEG * (seg + PAD), LANES), jnp.float32),
            pltpu.VMEM((ns, seg, SEG, LANES), jnp.float32),
            pltpu.VMEM((ns, seg, SEG, LANES), jnp.float32),
            pltpu.VMEM((ns, SEG * (seg + PAD), LANES), jnp.float32),
            pltpu.VMEM((ns, 1, LANES), jnp.float32),
        ],
        compiler_params=_params(("parallel", "parallel", "arbitrary"), 52),
        name="lru",
    )(x, g, w_in, w_in, conv_w, conv_b, w_r, b_r, w_i, b_i, lam)


def _residual_norm_kernel(a_ref, w_ref, r_ref, x1_ref, xn_ref):
    x1 = r_ref[...] + jnp.dot(a_ref[...], w_ref[...], preferred_element_type=jnp.float32)
    x1_ref[...] = x1
    ms = jnp.mean(x1 * x1, axis=-1, keepdims=True)
    xn_ref[...] = (x1 * lax.rsqrt(ms + EPS)).astype(xn_ref.dtype)


def _residual_norm(a, w, res, *, tm=512):
    m, k = a.shape
    d = w.shape[1]
    rows = pl.BlockSpec((tm, d), lambda i: (i, 0))
    return pl.pallas_call(
        _residual_norm_kernel,
        grid=(m // tm,),
        in_specs=[
            pl.BlockSpec((tm, k), lambda i: (i, 0)),
            pl.BlockSpec((k, d), lambda i: (0, 0), pipeline_mode=pl.Buffered(1)),
            rows,
        ],
        out_specs=[rows, rows],
        out_shape=[jax.ShapeDtypeStruct((m, d), jnp.float32),
                   jax.ShapeDtypeStruct((m, d), jnp.bfloat16)],
        compiler_params=_params(("parallel",), 48),
        name="residual_norm",
    )(a, w, res)


def _attention_kernel(q_ref, k_ref, v_ref, gate_ref, o_ref, q_s, z_s, w_s, c_ref, acc_ref, *, tk):
    i = pl.program_id(2)
    tq, dh = q_ref.shape
    nsub = tq // tk
    q_s[...] = (q_ref[...].astype(jnp.float32) * (LOG2E / math.sqrt(dh))).astype(jnp.bfloat16)

    rows = lax.broadcasted_iota(jnp.int32, (tk, tk), 0)
    cols = lax.broadcasted_iota(jnp.int32, (tk, tk), 1)
    tri = (rows >= cols).astype(jnp.bfloat16)

    c_ref[...] = jnp.zeros_like(c_ref)
    acc_ref[...] = jnp.zeros_like(acc_ref)

    def sub(a):
        return pl.ds(a * tk, tk)

    def key_block(d, a):
        blk = i * nsub + a - d
        return pl.ds(pl.multiple_of(jnp.maximum(blk, 0) * tk, tk), tk), blk >= 0

    def scores(q, kb):
        return lax.dot_general(q, kb, (((1,), (1,)), ((), ())),
                               preferred_element_type=jnp.float32)

    def weights(z, c, mask):
        sp = jnp.maximum(z, 0.0) + jnp.log(1.0 + jnp.exp2(-jnp.abs(z))) * LOG2E
        if mask is not None:
            sp = jnp.where(mask, sp, 0.0)
        incl = jnp.dot(sp.astype(jnp.bfloat16), tri, preferred_element_type=jnp.float32)
        w = jnp.exp2(z - incl - c)
        if mask is not None:
            w = jnp.where(mask, w, 0.0)
        return w.astype(jnp.bfloat16), incl[:, 0:1]

    def stage_scores(d, par):
        for a in range(nsub):
            keys, _ = key_block(d, a)
            z_s[par, sub(a), :] = scores(q_s[sub(a), :], k_ref[keys, :])

    def stage_weights(d, par, diagonal=False):
        if diagonal:
            row = lax.broadcasted_iota(jnp.int32, (tq, tk), 0) & (tk - 1)
            mask = lax.broadcasted_iota(jnp.int32, (tq, tk), 1) < row
        else:
            mask = None
            for a in range(nsub):
                _, exists = key_block(d, a)
                c_ref[sub(a), :] += jnp.where(exists, 0.0, DONE_BITS)
        w, total = weights(z_s[par], c_ref[...], mask)
        w_s[par] = w
        c_ref[...] += total

    def stage_values(d, par):
        for a in range(nsub):
            keys, _ = key_block(d, a)
            acc_ref[sub(a), :] += jnp.dot(w_s[par, sub(a), :], v_ref[keys, :],
                                          preferred_element_type=jnp.float32)

    stage_scores(0, 1)
    stage_scores(1, 0)
    stage_weights(0, 1, diagonal=True)
    stage_values(0, 1)
    stage_weights(1, 0)
    c_min1 = jnp.min(c_ref[...])

    @pl.when(c_min1 < SKIP_BITS)
    def _():
        stage_scores(2, 1)

    def body(carry):
        m, _ = carry
        d = 2 * m + 2
        stage_scores(d + 1, 0)
        stage_values(d - 1, 0)
        stage_weights(d, 1)
        c_min = jnp.min(c_ref[...])
        stage_scores(d + 2, 1)
        stage_values(d, 1)
        stage_weights(d + 1, 0)
        return m + 1, c_min

    def cond(carry):
        m, c_min = carry
        return jnp.logical_and(m < n_pairs, c_min < SKIP_BITS)

    n_pairs = (i * nsub + nsub - 2) // 2
    m_done, _ = lax.while_loop(cond, body, (jnp.int32(0), c_min1))
    stage_values(2 * m_done + 1, 0)
    o_ref[...] = (acc_ref[...] * _silu(gate_ref[...].astype(jnp.float32))).astype(o_ref.dtype)


def _attention(qg, kv, *, tq=4096, tk=256):
    bsz, s, _ = qg.shape
    tile = lambda part: pl.BlockSpec((None, tq, HEAD_DIM),
                                     lambda b, h, i: (b, i, part * N_HEADS + h))
    keys = lambda part: pl.BlockSpec((None, s, HEAD_DIM),
                                     lambda b, h, i: (b, 0, part * N_HEADS + h))
    return pl.pallas_call(
        functools.partial(_attention_kernel, tk=tk),
        grid=(bsz, N_HEADS, s // tq),
        in_specs=[tile(0), keys(0), keys(1), tile(1)],
        out_specs=pl.BlockSpec((None, tq, HEAD_DIM), lambda b, h, i: (b, i, h)),
        out_shape=jax.ShapeDtypeStruct((bsz, s, N_HEADS * HEAD_DIM), jnp.bfloat16),
        scratch_shapes=[
            pltpu.VMEM((tq, HEAD_DIM), jnp.bfloat16),
            pltpu.VMEM((2, tq, tk), jnp.float32),
            pltpu.VMEM((2, tq, tk), jnp.bfloat16),
            pltpu.VMEM((tq, 1), jnp.float32),
            pltpu.VMEM((tq, HEAD_DIM), jnp.float32),
        ],
        compiler_params=_params(("parallel", "parallel", "arbitrary"), 44),
        name="attention",
    )(qg, kv, kv, qg)


def _out_norm_kernel(a_ref, w_ref, x_ref, fn_ref, out_ref):
    y = x_ref[...] + jnp.dot(a_ref[...], w_ref[...], preferred_element_type=jnp.float32)
    ms = jnp.mean(y * y, axis=-1, keepdims=True)
    out_ref[...] = y * lax.rsqrt(ms + EPS) * fn_ref[...]


def _out_norm(a, w, x1, fn, *, tm=512):
    m, d = x1.shape
    rows = pl.BlockSpec((tm, d), lambda i: (i, 0))
    return pl.pallas_call(
        _out_norm_kernel,
        grid=(m // tm,),
        in_specs=[
            pl.BlockSpec((tm, a.shape[1]), lambda i: (i, 0)),
            pl.BlockSpec(w.shape, lambda i: (0, 0), pipeline_mode=pl.Buffered(1)),
            rows,
            pl.BlockSpec((1, d), lambda i: (0, 0)),
        ],
        out_specs=rows,
        out_shape=jax.ShapeDtypeStruct((m, d), jnp.float32),
        compiler_params=_params(("parallel",), 48),
        name="out_norm",
    )(a, w, x1, fn)


def kernel(x, a_norm, a_w_in, a_conv_w, a_conv_b, a_w_r, a_b_r, a_w_i, a_b_i, a_lambda,
           a_w_out, kv_norm, w_kv, b_norm, b_w_in, b_w_out, final_norm):
    bsz, s, d = x.shape
    m = bsz * s
    bf = jnp.bfloat16
    x2 = x.reshape(m, d)

    c = a_w_in.shape[2] // 2
    yg = _lru(x, a_norm[0:1], a_w_in[0].astype(bf), a_conv_w[0], a_conv_b[0:1],
              a_w_r[0].astype(bf), a_b_r[0:1], a_w_i[0].astype(bf), a_b_i[0:1],
              a_lambda[0:1])
    x1, xn = _residual_norm(yg.reshape(m, c), a_w_out[0].astype(bf), x2)
    kv = _matmul(xn, (kv_norm[:, None] * w_kv).astype(bf), bf, name="matmul_kv")
    qg = _matmul(xn, (b_norm[0][:, None] * b_w_in[0]).astype(bf), bf, name="matmul_qg")
    og = _attention(qg.reshape(bsz, s, -1), kv.reshape(bsz, s, -1))
    out = _out_norm(og.reshape(m, -1), b_w_out[0].astype(bf), x1, final_norm.reshape(1, d))
    return out.reshape(bsz, s, d)
```

```python
import functools
import math

import jax
import jax.numpy as jnp
from jax import lax
from jax.experimental import pallas as pl
from jax.experimental.pallas import tpu as pltpu

EPS = 1e-6
LOG2E = 1.0 / math.log(2.0)
LRU_C = 8.0
CONV_W = 4
LRU_BW = 256
N_HEADS = 16
HEAD_DIM = 128
SKIP_BITS = 200.0
DONE_BITS = 1.0e6

SUBLANES = 8
LANES = 128
SEG = SUBLANES
PAD = 4

MIB = 1024 * 1024


def _params(semantics, vmem_mib):
    return pltpu.CompilerParams(dimension_semantics=semantics,
                                vmem_limit_bytes=vmem_mib * MIB)


def _silu(g):
    return g * jax.nn.sigmoid(g)


def _matmul_heads_kernel(a_ref, w_ref, o_ref):
    res = jnp.dot(a_ref[...], w_ref[...], preferred_element_type=jnp.float32)
    for c in range(o_ref.shape[0]):
        o_ref[c] = res[:, c * HEAD_DIM:(c + 1) * HEAD_DIM].astype(o_ref.dtype)


def _matmul_heads(a, w, out_dtype, *, tm=1024, tn=2048, name):
    m, k = a.shape
    n = w.shape[1]
    return pl.pallas_call(
        _matmul_heads_kernel,
        grid=(m // tm, n // tn),
        in_specs=[
            pl.BlockSpec((tm, k), lambda i, j: (i, 0)),
            pl.BlockSpec((k, tn), lambda i, j: (0, j)),
        ],
        out_specs=pl.BlockSpec((tn // HEAD_DIM, tm, HEAD_DIM), lambda i, j: (j, i, 0)),
        out_shape=jax.ShapeDtypeStruct((n // HEAD_DIM, m, HEAD_DIM), out_dtype),
        compiler_params=_params(("parallel", "parallel"), 40),
        name=name,
    )(a, w)


def _lru_kernel(x_ref, g_ref, wx_ref, wg_ref, cw_ref, cb_ref, wr_ref, br_ref, wi_ref, bi_ref,
                lam_ref, o_ref, h_s, g_s, xext, a_s, u_s, hl_s, al_s, y_s, carry_s):
    t = pl.program_id(2)
    tt = x_ref.shape[0]
    cw = wx_ref.shape[1]
    ns = cw // LANES
    seg = tt // SEG
    pitch = seg + PAD

    x = x_ref[...]
    ms = jnp.mean(x * x, axis=-1, keepdims=True)
    h_s[...] = (x * lax.rsqrt(ms + EPS) * g_ref[...]).astype(h_s.dtype)

    @pl.when(t == 0)
    def _():
        xext[:, 0:SUBLANES, :] = jnp.zeros((ns, SUBLANES, LANES), jnp.float32)
        carry_s[...] = jnp.zeros_like(carry_s)

    @pl.when(t > 0)
    def _():
        xext[:, 0:SUBLANES, :] = xext[:, tt:tt + SUBLANES, :]

    def lanes(s):
        return slice(s * LANES, (s + 1) * LANES)

    nblk = cw // LRU_BW

    def project_x(n):
        xb = jnp.dot(h_s[...], wx_ref[:, n * LRU_BW:(n + 1) * LRU_BW],
                     preferred_element_type=jnp.float32)
        for j, s in enumerate(range(n * LRU_BW // LANES, (n + 1) * LRU_BW // LANES)):
            xext[s, SUBLANES:SUBLANES + tt, :] = xb[:, lanes(j)]

    def project_gate(n):
        g_s[:, n * LRU_BW:(n + 1) * LRU_BW] = jnp.dot(
            h_s[...], wg_ref[:, n * LRU_BW:(n + 1) * LRU_BW],
            preferred_element_type=jnp.float32)

    project_x(0)
    for n in range(nblk):
        slabs = range(n * LRU_BW // LANES, (n + 1) * LRU_BW // LANES)
        if n + 1 < nblk:
            project_x(n + 1)
        else:
            project_gate(0)
        parts = []
        for s in slabs:
            xc = cb_ref[:, lanes(s)]
            for w in range(CONV_W):
                off = SUBLANES - (CONV_W - 1) + w
                xc = xc + cw_ref[w:w + 1, lanes(s)] * xext[s, off:off + tt, :]
            parts.append(xc)
        xc = jnp.concatenate(parts, axis=1)
        sl = slice(n * LRU_BW, (n + 1) * LRU_BW)
        xn = xc.astype(jnp.bfloat16)
        r = jax.nn.sigmoid(
            jnp.dot(xn, wr_ref[n], preferred_element_type=jnp.float32) + br_ref[:, sl])
        i = jax.nn.sigmoid(
            jnp.dot(xn, wi_ref[n], preferred_element_type=jnp.float32) + bi_ref[:, sl])
        neg_lam = -lam_ref[:, sl]
        softplus = jnp.maximum(neg_lam, 0.0) + jnp.log1p(jnp.exp(-jnp.abs(neg_lam)))
        log_a = r * (-LRU_C * softplus)
        a = jnp.exp(log_a)
        one_minus_a2 = -jnp.tanh(log_a) * (a * a + 1.0)
        mult = jnp.where(one_minus_a2 > 0.0, one_minus_a2 * lax.rsqrt(one_minus_a2), 0.0)
        u = mult * (i * xc)
        for j, s in enumerate(slabs):
            for g in range(SEG):
                a_s[s, g * pitch:g * pitch + seg, :] = a[g * seg:(g + 1) * seg, lanes(j)]
                u_s[s, g * pitch:g * pitch + seg, :] = u[g * seg:(g + 1) * seg, lanes(j)]

    for n in range(1, nblk):
        project_gate(n)

    for s in range(ns):
        h = jnp.zeros((SEG, LANES), jnp.float32)
        prod = jnp.ones((SEG, LANES), jnp.float32)
        for p in range(seg):
            ap = a_s[s, pl.ds(p, SEG, stride=pitch), :]
            h = ap * h + u_s[s, pl.ds(p, SEG, stride=pitch), :]
            prod = ap * prod
            hl_s[s, p] = h
            al_s[s, p] = prod
        entering = [carry_s[s]]
        for g in range(SEG - 1):
            entering.append(prod[g:g + 1] * entering[-1] + h[g:g + 1])
        carry_s[s] = prod[SEG - 1:SEG] * entering[-1] + h[SEG - 1:SEG]
        h_in = jnp.concatenate(entering, axis=0)
        for p in range(seg):
            y_s[s, pl.ds(p, SEG, stride=pitch), :] = hl_s[s, p] + al_s[s, p] * h_in

    for s in range(ns):
        for g in range(SEG):
            rows = slice(g * seg, (g + 1) * seg)
            y = y_s[s, g * pitch:g * pitch + seg, :]
            o_ref[rows, lanes(s)] = (y * _silu(g_s[rows, lanes(s)])).astype(o_ref.dtype)


def _lru(x, g, w_in, conv_w, conv_b, w_r, b_r, w_i, b_i, lam, *, tt=256, cw=2560):
    bsz, s, d = x.shape
    c = w_in.shape[1] // 2
    nc = c // cw
    nb = cw // LRU_BW
    ns = cw // LANES
    seg = tt // SEG
    row = lambda b, ci, t: (0, ci)
    return pl.pallas_call(
        _lru_kernel,
        grid=(bsz, nc, s // tt),
        in_specs=[
            pl.BlockSpec((None, tt, d), lambda b, ci, t: (b, t, 0)),
            pl.BlockSpec((1, d), lambda b, ci, t: (0, 0)),
            pl.BlockSpec((d, cw), lambda b, ci, t: (0, ci), pipeline_mode=pl.Buffered(1)),
            pl.BlockSpec((d, cw), lambda b, ci, t: (0, nc + ci), pipeline_mode=pl.Buffered(1)),
            pl.BlockSpec((CONV_W, cw), row),
            pl.BlockSpec((1, cw), row),
            pl.BlockSpec((nb, LRU_BW, LRU_BW), lambda b, ci, t: (ci, 0, 0)),
            pl.BlockSpec((1, cw), row),
            pl.BlockSpec((nb, LRU_BW, LRU_BW), lambda b, ci, t: (ci, 0, 0)),
            pl.BlockSpec((1, cw), row),
            pl.BlockSpec((1, cw), row),
        ],
        out_specs=pl.BlockSpec((None, tt, cw), lambda b, ci, t: (b, t, ci)),
        out_shape=jax.ShapeDtypeStruct((bsz, s, c), jnp.bfloat16),
        scratch_shapes=[
            pltpu.VMEM((tt, d), jnp.bfloat16),
            pltpu.VMEM((tt, cw), jnp.float32),
            pltpu.VMEM((ns, tt + SUBLANES, LANES), jnp.float32),
            pltpu.VMEM((ns, SEG * (seg + PAD), LANES), jnp.float32),
            pltpu.VMEM((ns, SEG * (seg + PAD), LANES), jnp.float32),
            pltpu.VMEM((ns, seg, SEG, LANES), jnp.float32),
            pltpu.VMEM((ns, seg, SEG, LANES), jnp.float32),
            pltpu.VMEM((ns, SEG * (seg + PAD), LANES), jnp.float32),
            pltpu.VMEM((ns, 1, LANES), jnp.float32),
        ],
        compiler_params=_params(("parallel", "parallel", "arbitrary"), 52),
        name="lru",
    )(x, g, w_in, w_in, conv_w, conv_b, w_r, b_r, w_i, b_i, lam)


def _residual_norm_kernel(a_ref, w_ref, r_ref, x1_ref, xn_ref):
    x1 = r_ref[...] + jnp.dot(a_ref[...], w_ref[...], preferred_element_type=jnp.float32)
    x1_ref[...] = x1
    ms = jnp.mean(x1 * x1, axis=-1, keepdims=True)
    xn_ref[...] = (x1 * lax.rsqrt(ms + EPS)).astype(xn_ref.dtype)


def _residual_norm(a, w, res, *, tm=512):
    m, k = a.shape
    d = w.shape[1]
    rows = pl.BlockSpec((tm, d), lambda i: (i, 0))
    return pl.pallas_call(
        _residual_norm_kernel,
        grid=(m // tm,),
        in_specs=[
            pl.BlockSpec((tm, k), lambda i: (i, 0)),
            pl.BlockSpec((k, d), lambda i: (0, 0), pipeline_mode=pl.Buffered(1)),
            rows,
        ],
        out_specs=[rows, rows],
        out_shape=[jax.ShapeDtypeStruct((m, d), jnp.float32),
                   jax.ShapeDtypeStruct((m, d), jnp.bfloat16)],
        compiler_params=_params(("parallel",), 48),
        name="residual_norm",
    )(a, w, res)


def _attention_kernel(q_ref, k_ref, v_ref, gate_ref, o_ref, q_s, z_s, w_s, c_ref, acc_ref, *, tk):
    i = pl.program_id(2)
    tq, dh = q_ref.shape
    nsub = tq // tk
    q_s[...] = (q_ref[...].astype(jnp.float32) * (LOG2E / math.sqrt(dh))).astype(jnp.bfloat16)

    rows = lax.broadcasted_iota(jnp.int32, (tk, tk), 0)
    cols = lax.broadcasted_iota(jnp.int32, (tk, tk), 1)
    tri = (rows >= cols).astype(jnp.bfloat16)

    c_ref[...] = jnp.zeros_like(c_ref)
    acc_ref[...] = jnp.zeros_like(acc_ref)

    def sub(a):
        return pl.ds(a * tk, tk)

    def key_block(d, a):
        blk = i * nsub + a - d
        return pl.ds(pl.multiple_of(jnp.maximum(blk, 0) * tk, tk), tk), blk >= 0

    def scores(q, kb):
        return lax.dot_general(q, kb, (((1,), (1,)), ((), ())),
                               preferred_element_type=jnp.float32)

    def weights(z, c, mask):
        sp = jnp.maximum(z, 0.0) + jnp.log(1.0 + jnp.exp2(-jnp.abs(z))) * LOG2E
        if mask is not None:
            sp = jnp.where(mask, sp, 0.0)
        incl = jnp.dot(sp.astype(jnp.bfloat16), tri, preferred_element_type=jnp.float32)
        w = jnp.exp2(z - incl - c)
        if mask is not None:
            w = jnp.where(mask, w, 0.0)
        return w.astype(jnp.bfloat16), incl[:, 0:1]

    def stage_scores(d, par):
        for a in range(nsub):
            keys, _ = key_block(d, a)
            z_s[par, sub(a), :] = scores(q_s[sub(a), :], k_ref[keys, :])

    def stage_weights(d, par, diagonal=False):
        if diagonal:
            row = lax.broadcasted_iota(jnp.int32, (tq, tk), 0) & (tk - 1)
            mask = lax.broadcasted_iota(jnp.int32, (tq, tk), 1) < row
        else:
            mask = None
            for a in range(nsub):
                _, exists = key_block(d, a)
                c_ref[sub(a), :] += jnp.where(exists, 0.0, DONE_BITS)
        w, total = weights(z_s[par], c_ref[...], mask)
        w_s[par] = w
        c_ref[...] += total

    def stage_values(d, par):
        for a in range(nsub):
            keys, _ = key_block(d, a)
            acc_ref[sub(a), :] += jnp.dot(w_s[par, sub(a), :], v_ref[keys, :],
                                          preferred_element_type=jnp.float32)

    stage_scores(0, 1)
    stage_scores(1, 0)
    stage_weights(0, 1, diagonal=True)
    stage_values(0, 1)
    stage_weights(1, 0)
    c_min1 = jnp.min(c_ref[...])

    @pl.when(c_min1 < SKIP_BITS)
    def _():
        stage_scores(2, 1)

    def body(carry):
        m, _ = carry
        d = 2 * m + 2
        stage_scores(d + 1, 0)
        stage_values(d - 1, 0)
        stage_weights(d, 1)
        c_min = jnp.min(c_ref[...])
        stage_scores(d + 2, 1)
        stage_values(d, 1)
        stage_weights(d + 1, 0)
        return m + 1, c_min

    def cond(carry):
        m, c_min = carry
        return jnp.logical_and(m < n_pairs, c_min < SKIP_BITS)

    n_pairs = (i * nsub + nsub - 2) // 2
    m_done, _ = lax.while_loop(cond, body, (jnp.int32(0), c_min1))
    stage_values(2 * m_done + 1, 0)
    o_ref[...] = (acc_ref[...] * _silu(gate_ref[...].astype(jnp.float32))).astype(o_ref.dtype)


def _attention(qg, kv, bsz, *, tq=4096, tk=256):
    s = qg.shape[1] // bsz
    nq = s // tq
    tile = lambda part: pl.BlockSpec((None, tq, HEAD_DIM),
                                     lambda b, h, i: (part * N_HEADS + h, b * nq + i, 0))
    keys = lambda part: pl.BlockSpec((None, s, HEAD_DIM),
                                     lambda b, h, i: (part * N_HEADS + h, b, 0))
    return pl.pallas_call(
        functools.partial(_attention_kernel, tk=tk),
        grid=(bsz, N_HEADS, s // tq),
        in_specs=[tile(0), keys(0), keys(1), tile(1)],
        out_specs=pl.BlockSpec((None, tq, HEAD_DIM), lambda b, h, i: (b, i, h)),
        out_shape=jax.ShapeDtypeStruct((bsz, s, N_HEADS * HEAD_DIM), jnp.bfloat16),
        scratch_shapes=[
            pltpu.VMEM((tq, HEAD_DIM), jnp.bfloat16),
            pltpu.VMEM((2, tq, tk), jnp.float32),
            pltpu.VMEM((2, tq, tk), jnp.bfloat16),
            pltpu.VMEM((tq, 1), jnp.float32),
            pltpu.VMEM((tq, HEAD_DIM), jnp.float32),
        ],
        compiler_params=_params(("parallel", "parallel", "arbitrary"), 44),
        name="attention",
    )(qg, kv, kv, qg)


def _out_norm_kernel(a_ref, w_ref, x_ref, fn_ref, out_ref):
    y = x_ref[...] + jnp.dot(a_ref[...], w_ref[...], preferred_element_type=jnp.float32)
    ms = jnp.mean(y * y, axis=-1, keepdims=True)
    out_ref[...] = y * lax.rsqrt(ms + EPS) * fn_ref[...]


def _out_norm(a, w, x1, fn, *, tm=512):
    m, d = x1.shape
    rows = pl.BlockSpec((tm, d), lambda i: (i, 0))
    return pl.pallas_call(
        _out_norm_kernel,
        grid=(m // tm,),
        in_specs=[
            pl.BlockSpec((tm, a.shape[1]), lambda i: (i, 0)),
            pl.BlockSpec(w.shape, lambda i: (0, 0), pipeline_mode=pl.Buffered(1)),
            rows,
            pl.BlockSpec((1, d), lambda i: (0, 0)),
        ],
        out_specs=rows,
        out_shape=jax.ShapeDtypeStruct((m, d), jnp.float32),
        compiler_params=_params(("parallel",), 48),
        name="out_norm",
    )(a, w, x1, fn)


def kernel(x, a_norm, a_w_in, a_conv_w, a_conv_b, a_w_r, a_b_r, a_w_i, a_b_i, a_lambda,
           a_w_out, kv_norm, w_kv, b_norm, b_w_in, b_w_out, final_norm):
    bsz, s, d = x.shape
    m = bsz * s
    bf = jnp.bfloat16
    x2 = x.reshape(m, d)

    c = a_w_in.shape[2] // 2
    yg = _lru(x, a_norm[0:1], a_w_in[0].astype(bf), a_conv_w[0], a_conv_b[0:1],
              a_w_r[0].astype(bf), a_b_r[0:1], a_w_i[0].astype(bf), a_b_i[0:1],
              a_lambda[0:1])
    x1, xn = _residual_norm(yg.reshape(m, c), a_w_out[0].astype(bf), x2)
    kv = _matmul_heads(xn, (kv_norm[:, None] * w_kv).astype(bf), bf, name="matmul_kv")
    qg = _matmul_heads(xn, (b_norm[0][:, None] * b_w_in[0]).astype(bf), bf, name="matmul_qg")
    og = _attention(qg, kv, bsz)
    out = _out_norm(og.reshape(m, -1), b_w_out[0].astype(bf), x1, final_norm.reshape(1, d))
    return out.reshape(bsz, s, d)
```

```python
import functools
import math

import jax
import jax.numpy as jnp
from jax import lax
from jax.experimental import pallas as pl
from jax.experimental.pallas import tpu as pltpu

EPS = 1e-6
LOG2E = 1.0 / math.log(2.0)
LRU_C = 8.0
CONV_W = 4
LRU_BW = 256
N_HEADS = 16
HEAD_DIM = 128
SKIP_BITS = 200.0
DONE_BITS = 1.0e6

SUBLANES = 8
LANES = 128
SEG = SUBLANES
PAD = 4

MIB = 1024 * 1024


def _params(semantics, vmem_mib):
    return pltpu.CompilerParams(dimension_semantics=semantics,
                                vmem_limit_bytes=vmem_mib * MIB)


def _silu(g):
    return g * jax.nn.sigmoid(g)


def _matmul_heads_kernel(a_ref, w_ref, o_ref):
    res = jnp.dot(a_ref[...], w_ref[...], preferred_element_type=jnp.float32)
    for c in range(o_ref.shape[0]):
        o_ref[c] = res[:, c * HEAD_DIM:(c + 1) * HEAD_DIM].astype(o_ref.dtype)


def _matmul_heads(a, w, out_dtype, *, tm=1024, tn=2048, name):
    m, k = a.shape
    n = w.shape[1]
    return pl.pallas_call(
        _matmul_heads_kernel,
        grid=(m // tm, n // tn),
        in_specs=[
            pl.BlockSpec((tm, k), lambda i, j: (i, 0)),
            pl.BlockSpec((k, tn), lambda i, j: (0, j)),
        ],
        out_specs=pl.BlockSpec((tn // HEAD_DIM, tm, HEAD_DIM), lambda i, j: (j, i, 0)),
        out_shape=jax.ShapeDtypeStruct((n // HEAD_DIM, m, HEAD_DIM), out_dtype),
        compiler_params=_params(("parallel", "parallel"), 40),
        name=name,
    )(a, w)


def _lru_kernel(x_ref, g_ref, wx_ref, wg_ref, cw_ref, cb_ref, wr_ref, br_ref, wi_ref, bi_ref,
                lam_ref, o_ref, h_s, g_s, xext, a_s, u_s, hl_s, al_s, y_s, carry_s):
    t = pl.program_id(2)
    tt = x_ref.shape[0]
    cw = wx_ref.shape[1]
    ns = cw // LANES
    seg = tt // SEG
    pitch = seg + PAD

    x = x_ref[...]
    ms = jnp.mean(x * x, axis=-1, keepdims=True)
    h_s[...] = (x * lax.rsqrt(ms + EPS) * g_ref[...]).astype(h_s.dtype)

    @pl.when(t == 0)
    def _():
        xext[:, 0:SUBLANES, :] = jnp.zeros((ns, SUBLANES, LANES), jnp.float32)
        carry_s[...] = jnp.zeros_like(carry_s)

    @pl.when(t > 0)
    def _():
        xext[:, 0:SUBLANES, :] = xext[:, tt:tt + SUBLANES, :]

    def lanes(s):
        return slice(s * LANES, (s + 1) * LANES)

    nblk = cw // LRU_BW

    def project_x(n):
        xb = jnp.dot(h_s[...], wx_ref[:, n * LRU_BW:(n + 1) * LRU_BW],
                     preferred_element_type=jnp.float32)
        for j, s in enumerate(range(n * LRU_BW // LANES, (n + 1) * LRU_BW // LANES)):
            xext[s, SUBLANES:SUBLANES + tt, :] = xb[:, lanes(j)]

    def project_gate(n):
        g_s[:, n * LRU_BW:(n + 1) * LRU_BW] = jnp.dot(
            h_s[...], wg_ref[:, n * LRU_BW:(n + 1) * LRU_BW],
            preferred_element_type=jnp.float32)

    project_x(0)
    for n in range(nblk):
        slabs = range(n * LRU_BW // LANES, (n + 1) * LRU_BW // LANES)
        if n + 1 < nblk:
            project_x(n + 1)
        else:
            project_gate(0)
        parts = []
        for s in slabs:
            xc = cb_ref[:, lanes(s)]
            for w in range(CONV_W):
                off = SUBLANES - (CONV_W - 1) + w
                xc = xc + cw_ref[w:w + 1, lanes(s)] * xext[s, off:off + tt, :]
            parts.append(xc)
        xc = jnp.concatenate(parts, axis=1)
        sl = slice(n * LRU_BW, (n + 1) * LRU_BW)
        xn = xc.astype(jnp.bfloat16)
        r = jax.nn.sigmoid(
            jnp.dot(xn, wr_ref[n], preferred_element_type=jnp.float32) + br_ref[:, sl])
        i = jax.nn.sigmoid(
            jnp.dot(xn, wi_ref[n], preferred_element_type=jnp.float32) + bi_ref[:, sl])
        neg_lam = -lam_ref[:, sl]
        softplus = jnp.maximum(neg_lam, 0.0) + jnp.log1p(jnp.exp(-jnp.abs(neg_lam)))
        log_a = r * (-LRU_C * softplus)
        a = jnp.exp(log_a)
        one_minus_a2 = -jnp.tanh(log_a) * (a * a + 1.0)
        mult = jnp.where(one_minus_a2 > 0.0, one_minus_a2 * lax.rsqrt(one_minus_a2), 0.0)
        u = mult * (i * xc)
        for j, s in enumerate(slabs):
            for g in range(SEG):
                a_s[s, g * pitch:g * pitch + seg, :] = a[g * seg:(g + 1) * seg, lanes(j)]
                u_s[s, g * pitch:g * pitch + seg, :] = u[g * seg:(g + 1) * seg, lanes(j)]

    for n in range(1, nblk):
        project_gate(n)

    for s in range(ns):
        h = jnp.zeros((SEG, LANES), jnp.float32)
        prod = jnp.ones((SEG, LANES), jnp.float32)
        for p in range(seg):
            ap = a_s[s, pl.ds(p, SEG, stride=pitch), :]
            h = ap * h + u_s[s, pl.ds(p, SEG, stride=pitch), :]
            prod = ap * prod
            hl_s[s, p] = h
            al_s[s, p] = prod
        entering = [carry_s[s]]
        for g in range(SEG - 1):
            entering.append(prod[g:g + 1] * entering[-1] + h[g:g + 1])
        carry_s[s] = prod[SEG - 1:SEG] * entering[-1] + h[SEG - 1:SEG]
        h_in = jnp.concatenate(entering, axis=0)
        for p in range(seg):
            y_s[s, pl.ds(p, SEG, stride=pitch), :] = hl_s[s, p] + al_s[s, p] * h_in

    for s in range(ns):
        for g in range(SEG):
            rows = slice(g * seg, (g + 1) * seg)
            y = y_s[s, g * pitch:g * pitch + seg, :]
            o_ref[rows, lanes(s)] = (y * _silu(g_s[rows, lanes(s)])).astype(o_ref.dtype)


def _lru(x, g, w_in, conv_w, conv_b, w_r, b_r, w_i, b_i, lam, *, tt=256, cw=2560):
    bsz, s, d = x.shape
    c = w_in.shape[1] // 2
    nc = c // cw
    nb = cw // LRU_BW
    ns = cw // LANES
    seg = tt // SEG
    row = lambda b, ci, t: (0, ci)
    return pl.pallas_call(
        _lru_kernel,
        grid=(bsz, nc, s // tt),
        in_specs=[
            pl.BlockSpec((None, tt, d), lambda b, ci, t: (b, t, 0)),
            pl.BlockSpec((1, d), lambda b, ci, t: (0, 0)),
            pl.BlockSpec((d, cw), lambda b, ci, t: (0, ci), pipeline_mode=pl.Buffered(1)),
            pl.BlockSpec((d, cw), lambda b, ci, t: (0, nc + ci), pipeline_mode=pl.Buffered(1)),
            pl.BlockSpec((CONV_W, cw), row),
            pl.BlockSpec((1, cw), row),
            pl.BlockSpec((nb, LRU_BW, LRU_BW), lambda b, ci, t: (ci, 0, 0)),
            pl.BlockSpec((1, cw), row),
            pl.BlockSpec((nb, LRU_BW, LRU_BW), lambda b, ci, t: (ci, 0, 0)),
            pl.BlockSpec((1, cw), row),
            pl.BlockSpec((1, cw), row),
        ],
        out_specs=pl.BlockSpec((None, tt, cw), lambda b, ci, t: (b, t, ci)),
        out_shape=jax.ShapeDtypeStruct((bsz, s, c), jnp.bfloat16),
        scratch_shapes=[
            pltpu.VMEM((tt, d), jnp.bfloat16),
            pltpu.VMEM((tt, cw), jnp.float32),
            pltpu.VMEM((ns, tt + SUBLANES, LANES), jnp.float32),
            pltpu.VMEM((ns, SEG * (seg + PAD), LANES), jnp.float32),
            pltpu.VMEM((ns, SEG * (seg + PAD), LANES), jnp.float32),
            pltpu.VMEM((ns, seg, SEG, LANES), jnp.float32),
            pltpu.VMEM((ns, seg, SEG, LANES), jnp.float32),
            pltpu.VMEM((ns, SEG * (seg + PAD), LANES), jnp.float32),
            pltpu.VMEM((ns, 1, LANES), jnp.float32),
        ],
        compiler_params=_params(("parallel", "parallel", "arbitrary"), 52),
        name="lru",
    )(x, g, w_in, w_in, conv_w, conv_b, w_r, b_r, w_i, b_i, lam)


def _residual_norm_kernel(a_ref, w_ref, r_ref, x1_ref, xn_ref):
    x1 = r_ref[...] + jnp.dot(a_ref[...], w_ref[...], preferred_element_type=jnp.float32)
    x1_ref[...] = x1
    ms = jnp.mean(x1 * x1, axis=-1, keepdims=True)
    xn_ref[...] = (x1 * lax.rsqrt(ms + EPS)).astype(xn_ref.dtype)


def _residual_norm(a, w, res, *, tm=512):
    m, k = a.shape
    d = w.shape[1]
    rows = pl.BlockSpec((tm, d), lambda i: (i, 0))
    return pl.pallas_call(
        _residual_norm_kernel,
        grid=(m // tm,),
        in_specs=[
            pl.BlockSpec((tm, k), lambda i: (i, 0)),
            pl.BlockSpec((k, d), lambda i: (0, 0), pipeline_mode=pl.Buffered(1)),
            rows,
        ],
        out_specs=[rows, rows],
        out_shape=[jax.ShapeDtypeStruct((m, d), jnp.float32),
                   jax.ShapeDtypeStruct((m, d), jnp.bfloat16)],
        compiler_params=_params(("parallel",), 48),
        name="residual_norm",
    )(a, w, res)


def _attention_kernel(q_ref, k_ref, v_ref, gate_ref, o_ref, q_s, z_s, w_s, c_ref, acc_ref, *, tk):
    i = pl.program_id(2)
    tq, dh = q_ref.shape
    nsub = tq // tk
    q_s[...] = (q_ref[...].astype(jnp.float32) * (LOG2E / math.sqrt(dh))).astype(jnp.bfloat16)

    rows = lax.broadcasted_iota(jnp.int32, (tk, tk), 0)
    cols = lax.broadcasted_iota(jnp.int32, (tk, tk), 1)
    tri = (rows >= cols).astype(jnp.bfloat16)

    c_ref[...] = jnp.zeros_like(c_ref)
    acc_ref[...] = jnp.zeros_like(acc_ref)

    def sub(a):
        return pl.ds(a * tk, tk)

    def key_block(d, a):
        blk = i * nsub + a - d
        return pl.ds(pl.multiple_of(jnp.maximum(blk, 0) * tk, tk), tk), blk >= 0

    def scores(q, kb):
        return lax.dot_general(q, kb, (((1,), (1,)), ((), ())),
                               preferred_element_type=jnp.float32)

    def weights(z, c, mask):
        sp = jnp.maximum(z, 0.0) + jnp.log(1.0 + jnp.exp2(-jnp.abs(z))) * LOG2E
        if mask is not None:
            sp = jnp.where(mask, sp, 0.0)
        incl = jnp.dot(sp.astype(jnp.bfloat16), tri, preferred_element_type=jnp.float32)
        w = jnp.exp2(z - incl - c)
        if mask is not None:
            w = jnp.where(mask, w, 0.0)
        return w.astype(jnp.bfloat16), incl[:, 0:1]

    def stage_scores(d, par):
        for a in range(nsub):
            keys, _ = key_block(d, a)
            z_s[par, sub(a), :] = scores(q_s[sub(a), :], k_ref[keys, :])

    def stage_weights(d, par, diagonal=False):
        if diagonal:
            row = lax.broadcasted_iota(jnp.int32, (tq, tk), 0) & (tk - 1)
            mask = lax.broadcasted_iota(jnp.int32, (tq, tk), 1) < row
        else:
            mask = None
            for a in range(nsub):
                _, exists = key_block(d, a)
                c_ref[sub(a), :] += jnp.where(exists, 0.0, DONE_BITS)
        w, total = weights(z_s[par], c_ref[...], mask)
        w_s[par] = w
        c_ref[...] += total

    def stage_values(d, par):
        for a in range(nsub):
            keys, _ = key_block(d, a)
            acc_ref[sub(a), :] += jnp.dot(w_s[par, sub(a), :], v_ref[keys, :],
                                          preferred_element_type=jnp.float32)

    stage_scores(0, 1)
    stage_scores(1, 0)
    stage_weights(0, 1, diagonal=True)
    stage_values(0, 1)
    stage_weights(1, 0)
    c_min1 = jnp.min(c_ref[...])

    @pl.when(c_min1 < SKIP_BITS)
    def _():
        stage_scores(2, 1)

    def body(carry):
        m, _ = carry
        d = 2 * m + 2
        stage_scores(d + 1, 0)
        stage_values(d - 1, 0)
        stage_weights(d, 1)
        c_min = jnp.min(c_ref[...])
        stage_scores(d + 2, 1)
        stage_values(d, 1)
        stage_weights(d + 1, 0)
        return m + 1, c_min

    def cond(carry):
        m, c_min = carry
        return jnp.logical_and(m < n_pairs, c_min < SKIP_BITS)

    n_pairs = (i * nsub + nsub - 2) // 2
    m_done, _ = lax.while_loop(cond, body, (jnp.int32(0), c_min1))
    stage_values(2 * m_done + 1, 0)
    o_ref[...] = (acc_ref[...] * _silu(gate_ref[...].astype(jnp.float32))).astype(o_ref.dtype)


def _attention(qg, kv, bsz, *, tq=4096, tk=256):
    s = qg.shape[1] // bsz
    nq = s // tq
    tile = lambda part: pl.BlockSpec((None, tq, HEAD_DIM),
                                     lambda b, h, i: (part * N_HEADS + h, b * nq + i, 0))
    keys = lambda part: pl.BlockSpec((None, s, HEAD_DIM),
                                     lambda b, h, i: (part * N_HEADS + h, b, 0))
    return pl.pallas_call(
        functools.partial(_attention_kernel, tk=tk),
        grid=(bsz, N_HEADS, s // tq),
        in_specs=[tile(0), keys(0), keys(1), tile(1)],
        out_specs=pl.BlockSpec((None, tq, HEAD_DIM), lambda b, h, i: (h, b * nq + i, 0)),
        out_shape=jax.ShapeDtypeStruct((N_HEADS, bsz * s, HEAD_DIM), jnp.bfloat16),
        scratch_shapes=[
            pltpu.VMEM((tq, HEAD_DIM), jnp.bfloat16),
            pltpu.VMEM((2, tq, tk), jnp.float32),
            pltpu.VMEM((2, tq, tk), jnp.bfloat16),
            pltpu.VMEM((tq, 1), jnp.float32),
            pltpu.VMEM((tq, HEAD_DIM), jnp.float32),
        ],
        compiler_params=_params(("parallel", "parallel", "arbitrary"), 44),
        name="attention",
    )(qg, kv, kv, qg)


def _out_norm_kernel(a_ref, w_ref, x_ref, fn_ref, out_ref):
    a = jnp.concatenate([a_ref[h] for h in range(a_ref.shape[0])], axis=1)
    y = x_ref[...] + jnp.dot(a, w_ref[...], preferred_element_type=jnp.float32)
    ms = jnp.mean(y * y, axis=-1, keepdims=True)
    out_ref[...] = y * lax.rsqrt(ms + EPS) * fn_ref[...]


def _out_norm(a, w, x1, fn, *, tm=512):
    m, d = x1.shape
    rows = pl.BlockSpec((tm, d), lambda i: (i, 0))
    return pl.pallas_call(
        _out_norm_kernel,
        grid=(m // tm,),
        in_specs=[
            pl.BlockSpec((a.shape[0], tm, a.shape[2]), lambda i: (0, i, 0)),
            pl.BlockSpec(w.shape, lambda i: (0, 0), pipeline_mode=pl.Buffered(1)),
            rows,
            pl.BlockSpec((1, d), lambda i: (0, 0)),
        ],
        out_specs=rows,
        out_shape=jax.ShapeDtypeStruct((m, d), jnp.float32),
        compiler_params=_params(("parallel",), 48),
        name="out_norm",
    )(a, w, x1, fn)


def kernel(x, a_norm, a_w_in, a_conv_w, a_conv_b, a_w_r, a_b_r, a_w_i, a_b_i, a_lambda,
           a_w_out, kv_norm, w_kv, b_norm, b_w_in, b_w_out, final_norm):
    bsz, s, d = x.shape
    m = bsz * s
    bf = jnp.bfloat16
    x2 = x.reshape(m, d)

    c = a_w_in.shape[2] // 2
    yg = _lru(x, a_norm[0:1], a_w_in[0].astype(bf), a_conv_w[0], a_conv_b[0:1],
              a_w_r[0].astype(bf), a_b_r[0:1], a_w_i[0].astype(bf), a_b_i[0:1],
              a_lambda[0:1])
    x1, xn = _residual_norm(yg.reshape(m, c), a_w_out[0].astype(bf), x2)
    kv = _matmul_heads(xn, (kv_norm[:, None] * w_kv).astype(bf), bf, name="matmul_kv")
    qg = _matmul_heads(xn, (b_norm[0][:, None] * b_w_in[0]).astype(bf), bf, name="matmul_qg")
    og = _attention(qg, kv, bsz)
    out = _out_norm(og, b_w_out[0].astype(bf), x1, final_norm.reshape(1, d))
    return out.reshape(bsz, s, d)
```

```python
import functools
import math

import jax
import jax.numpy as jnp
from jax import lax
from jax.experimental import pallas as pl
from jax.experimental.pallas import tpu as pltpu

EPS = 1e-6
LOG2E = 1.0 / math.log(2.0)
LRU_C = 8.0
CONV_W = 4
LRU_BW = 256
N_HEADS = 16
HEAD_DIM = 128
SKIP_BITS = 200.0
DONE_BITS = 1.0e6

SUBLANES = 8
LANES = 128
SEG = SUBLANES
PAD = 4

MIB = 1024 * 1024


def _params(semantics, vmem_mib):
    return pltpu.CompilerParams(dimension_semantics=semantics,
                                vmem_limit_bytes=vmem_mib * MIB)


def _silu(g):
    return g * jax.nn.sigmoid(g)


def _matmul_kernel(a_ref, w_ref, o_ref):
    o_ref[...] = jnp.dot(a_ref[...], w_ref[...],
                         preferred_element_type=jnp.float32).astype(o_ref.dtype)


def _matmul(a, w, out_dtype, *, tm=1024, tn=2048, name):
    m, k = a.shape
    n = w.shape[1]
    return pl.pallas_call(
        _matmul_kernel,
        grid=(m // tm, n // tn),
        in_specs=[
            pl.BlockSpec((tm, k), lambda i, j: (i, 0)),
            pl.BlockSpec((k, tn), lambda i, j: (0, j)),
        ],
        out_specs=pl.BlockSpec((tm, tn), lambda i, j: (i, j)),
        out_shape=jax.ShapeDtypeStruct((m, n), out_dtype),
        compiler_params=_params(("parallel", "parallel"), 40),
        name=name,
    )(a, w)


def _matmul_heads_kernel(a_ref, w_ref, o_ref):
    res = jnp.dot(a_ref[...], w_ref[...], preferred_element_type=jnp.float32)
    for c in range(o_ref.shape[0]):
        o_ref[c] = res[:, c * HEAD_DIM:(c + 1) * HEAD_DIM].astype(o_ref.dtype)


def _matmul_heads(a, w, out_dtype, *, tm=1024, tn=2048, name):
    m, k = a.shape
    n = w.shape[1]
    return pl.pallas_call(
        _matmul_heads_kernel,
        grid=(m // tm, n // tn),
        in_specs=[
            pl.BlockSpec((tm, k), lambda i, j: (i, 0)),
            pl.BlockSpec((k, tn), lambda i, j: (0, j)),
        ],
        out_specs=pl.BlockSpec((tn // HEAD_DIM, tm, HEAD_DIM), lambda i, j: (j, i, 0)),
        out_shape=jax.ShapeDtypeStruct((n // HEAD_DIM, m, HEAD_DIM), out_dtype),
        compiler_params=_params(("parallel", "parallel"), 40),
        name=name,
    )(a, w)


def _lru_kernel(x_ref, g_ref, wx_ref, wg_ref, cw_ref, cb_ref, wr_ref, br_ref, wi_ref, bi_ref,
                lam_ref, o_ref, h_s, g_s, xext, a_s, u_s, hl_s, al_s, y_s, carry_s):
    t = pl.program_id(2)
    tt = x_ref.shape[0]
    cw = wx_ref.shape[1]
    ns = cw // LANES
    seg = tt // SEG
    pitch = seg + PAD

    x = x_ref[...]
    ms = jnp.mean(x * x, axis=-1, keepdims=True)
    h_s[...] = (x * lax.rsqrt(ms + EPS) * g_ref[...]).astype(h_s.dtype)

    @pl.when(t == 0)
    def _():
        xext[:, 0:SUBLANES, :] = jnp.zeros((ns, SUBLANES, LANES), jnp.float32)
        carry_s[...] = jnp.zeros_like(carry_s)

    @pl.when(t > 0)
    def _():
        xext[:, 0:SUBLANES, :] = xext[:, tt:tt + SUBLANES, :]

    def lanes(s):
        return slice(s * LANES, (s + 1) * LANES)

    nblk = cw // LRU_BW

    def project_x(n):
        xb = jnp.dot(h_s[...], wx_ref[:, n * LRU_BW:(n + 1) * LRU_BW],
                     preferred_element_type=jnp.float32)
        for j, s in enumerate(range(n * LRU_BW // LANES, (n + 1) * LRU_BW // LANES)):
            xext[s, SUBLANES:SUBLANES + tt, :] = xb[:, lanes(j)]

    def project_gate(n):
        g_s[:, n * LRU_BW:(n + 1) * LRU_BW] = jnp.dot(
            h_s[...], wg_ref[:, n * LRU_BW:(n + 1) * LRU_BW],
            preferred_element_type=jnp.float32)

    project_x(0)
    for n in range(nblk):
        slabs = range(n * LRU_BW // LANES, (n + 1) * LRU_BW // LANES)
        if n + 1 < nblk:
            project_x(n + 1)
        else:
            project_gate(0)
        parts = []
        for s in slabs:
            xc = cb_ref[:, lanes(s)]
            for w in range(CONV_W):
                off = SUBLANES - (CONV_W - 1) + w
                xc = xc + cw_ref[w:w + 1, lanes(s)] * xext[s, off:off + tt, :]
            parts.append(xc)
        xc = jnp.concatenate(parts, axis=1)
        sl = slice(n * LRU_BW, (n + 1) * LRU_BW)
        xn = xc.astype(jnp.bfloat16)
        r = jax.nn.sigmoid(
            jnp.dot(xn, wr_ref[n], preferred_element_type=jnp.float32) + br_ref[:, sl])
        i = jax.nn.sigmoid(
            jnp.dot(xn, wi_ref[n], preferred_element_type=jnp.float32) + bi_ref[:, sl])
        neg_lam = -lam_ref[:, sl]
        softplus = jnp.maximum(neg_lam, 0.0) + jnp.log1p(jnp.exp(-jnp.abs(neg_lam)))
        log_a = r * (-LRU_C * softplus)
        a = jnp.exp(log_a)
        one_minus_a2 = -jnp.tanh(log_a) * (a * a + 1.0)
        mult = jnp.where(one_minus_a2 > 0.0, one_minus_a2 * lax.rsqrt(one_minus_a2), 0.0)
        u = mult * (i * xc)
        for j, s in enumerate(slabs):
            for g in range(SEG):
                a_s[s, g * pitch:g * pitch + seg, :] = a[g * seg:(g + 1) * seg, lanes(j)]
                u_s[s, g * pitch:g * pitch + seg, :] = u[g * seg:(g + 1) * seg, lanes(j)]

    for n in range(1, nblk):
        project_gate(n)

    for s in range(ns):
        h = jnp.zeros((SEG, LANES), jnp.float32)
        prod = jnp.ones((SEG, LANES), jnp.float32)
        for p in range(seg):
            ap = a_s[s, pl.ds(p, SEG, stride=pitch), :]
            h = ap * h + u_s[s, pl.ds(p, SEG, stride=pitch), :]
            prod = ap * prod
            hl_s[s, p] = h
            al_s[s, p] = prod
        entering = [carry_s[s]]
        for g in range(SEG - 1):
            entering.append(prod[g:g + 1] * entering[-1] + h[g:g + 1])
        carry_s[s] = prod[SEG - 1:SEG] * entering[-1] + h[SEG - 1:SEG]
        h_in = jnp.concatenate(entering, axis=0)
        for p in range(seg):
            y_s[s, pl.ds(p, SEG, stride=pitch), :] = hl_s[s, p] + al_s[s, p] * h_in

    for s in range(ns):
        for g in range(SEG):
            rows = slice(g * seg, (g + 1) * seg)
            y = y_s[s, g * pitch:g * pitch + seg, :]
            o_ref[rows, lanes(s)] = (y * _silu(g_s[rows, lanes(s)])).astype(o_ref.dtype)


def _lru(x, g, w_in, conv_w, conv_b, w_r, b_r, w_i, b_i, lam, *, tt=256, cw=2560):
    bsz, s, d = x.shape
    c = w_in.shape[1] // 2
    nc = c // cw
    nb = cw // LRU_BW
    ns = cw // LANES
    seg = tt // SEG
    row = lambda b, ci, t: (0, ci)
    return pl.pallas_call(
        _lru_kernel,
        grid=(bsz, nc, s // tt),
        in_specs=[
            pl.BlockSpec((None, tt, d), lambda b, ci, t: (b, t, 0)),
            pl.BlockSpec((1, d), lambda b, ci, t: (0, 0)),
            pl.BlockSpec((d, cw), lambda b, ci, t: (0, ci), pipeline_mode=pl.Buffered(1)),
            pl.BlockSpec((d, cw), lambda b, ci, t: (0, nc + ci), pipeline_mode=pl.Buffered(1)),
            pl.BlockSpec((CONV_W, cw), row),
            pl.BlockSpec((1, cw), row),
            pl.BlockSpec((nb, LRU_BW, LRU_BW), lambda b, ci, t: (ci, 0, 0)),
            pl.BlockSpec((1, cw), row),
            pl.BlockSpec((nb, LRU_BW, LRU_BW), lambda b, ci, t: (ci, 0, 0)),
            pl.BlockSpec((1, cw), row),
            pl.BlockSpec((1, cw), row),
        ],
        out_specs=pl.BlockSpec((None, tt, cw), lambda b, ci, t: (b, t, ci)),
        out_shape=jax.ShapeDtypeStruct((bsz, s, c), jnp.bfloat16),
        scratch_shapes=[
            pltpu.VMEM((tt, d), jnp.bfloat16),
            pltpu.VMEM((tt, cw), jnp.float32),
            pltpu.VMEM((ns, tt + SUBLANES, LANES), jnp.float32),
            pltpu.VMEM((ns, SEG * (seg + PAD), LANES), jnp.float32),
            pltpu.VMEM((ns, SEG * (seg + PAD), LANES), jnp.float32),
            pltpu.VMEM((ns, seg, SEG, LANES), jnp.float32),
            pltpu.VMEM((ns, seg, SEG, LANES), jnp.float32),
            pltpu.VMEM((ns, SEG * (seg + PAD), LANES), jnp.float32),
            pltpu.VMEM((ns, 1, LANES), jnp.float32),
        ],
        compiler_params=_params(("parallel", "parallel", "arbitrary"), 52),
        name="lru",
    )(x, g, w_in, w_in, conv_w, conv_b, w_r, b_r, w_i, b_i, lam)


def _residual_norm_kernel(a_ref, w_ref, r_ref, x1_ref, xn_ref):
    x1 = r_ref[...] + jnp.dot(a_ref[...], w_ref[...], preferred_element_type=jnp.float32)
    x1_ref[...] = x1
    ms = jnp.mean(x1 * x1, axis=-1, keepdims=True)
    xn_ref[...] = (x1 * lax.rsqrt(ms + EPS)).astype(xn_ref.dtype)


def _residual_norm(a, w, res, *, tm=512):
    m, k = a.shape
    d = w.shape[1]
    rows = pl.BlockSpec((tm, d), lambda i: (i, 0))
    return pl.pallas_call(
        _residual_norm_kernel,
        grid=(m // tm,),
        in_specs=[
            pl.BlockSpec((tm, k), lambda i: (i, 0)),
            pl.BlockSpec((k, d), lambda i: (0, 0), pipeline_mode=pl.Buffered(1)),
            rows,
        ],
        out_specs=[rows, rows],
        out_shape=[jax.ShapeDtypeStruct((m, d), jnp.float32),
                   jax.ShapeDtypeStruct((m, d), jnp.bfloat16)],
        compiler_params=_params(("parallel",), 48),
        name="residual_norm",
    )(a, w, res)


def _attention_kernel(q_ref, k_ref, v_ref, gate_ref, o_ref, q_s, z_s, w_s, c_ref, acc_ref, *, tk):
    i = pl.program_id(2)
    tq, dh = q_ref.shape
    nsub = tq // tk
    q_s[...] = (q_ref[...].astype(jnp.float32) * (LOG2E / math.sqrt(dh))).astype(jnp.bfloat16)

    rows = lax.broadcasted_iota(jnp.int32, (tk, tk), 0)
    cols = lax.broadcasted_iota(jnp.int32, (tk, tk), 1)
    tri = (rows >= cols).astype(jnp.bfloat16)

    c_ref[...] = jnp.zeros_like(c_ref)
    acc_ref[...] = jnp.zeros_like(acc_ref)

    def sub(a):
        return pl.ds(a * tk, tk)

    def key_block(d, a):
        blk = i * nsub + a - d
        return pl.ds(pl.multiple_of(jnp.maximum(blk, 0) * tk, tk), tk), blk >= 0

    def scores(q, kb):
        return lax.dot_general(q, kb, (((1,), (1,)), ((), ())),
                               preferred_element_type=jnp.float32)

    def weights(z, c, mask):
        sp = jnp.maximum(z, 0.0) + jnp.log(1.0 + jnp.exp2(-jnp.abs(z))) * LOG2E
        if mask is not None:
            sp = jnp.where(mask, sp, 0.0)
        incl = jnp.dot(sp.astype(jnp.bfloat16), tri, preferred_element_type=jnp.float32)
        w = jnp.exp2(z - incl - c)
        if mask is not None:
            w = jnp.where(mask, w, 0.0)
        return w.astype(jnp.bfloat16), incl[:, 0:1]

    def stage_scores(d, par):
        for a in range(nsub):
            keys, _ = key_block(d, a)
            z_s[par, sub(a), :] = scores(q_s[sub(a), :], k_ref[keys, :])

    def stage_weights(d, par, diagonal=False):
        if diagonal:
            row = lax.broadcasted_iota(jnp.int32, (tq, tk), 0) & (tk - 1)
            mask = lax.broadcasted_iota(jnp.int32, (tq, tk), 1) < row
        else:
            mask = None
            for a in range(nsub):
                _, exists = key_block(d, a)
                c_ref[sub(a), :] += jnp.where(exists, 0.0, DONE_BITS)
        w, total = weights(z_s[par], c_ref[...], mask)
        w_s[par] = w
        c_ref[...] += total

    def stage_values(d, par):
        for a in range(nsub):
            keys, _ = key_block(d, a)
            acc_ref[sub(a), :] += jnp.dot(w_s[par, sub(a), :], v_ref[keys, :],
                                          preferred_element_type=jnp.float32)

    stage_scores(0, 1)
    stage_scores(1, 0)
    stage_weights(0, 1, diagonal=True)
    stage_values(0, 1)
    stage_weights(1, 0)
    c_min1 = jnp.min(c_ref[...])

    @pl.when(c_min1 < SKIP_BITS)
    def _():
        stage_scores(2, 1)

    def body(carry):
        m, _ = carry
        d = 2 * m + 2
        stage_scores(d + 1, 0)
        stage_values(d - 1, 0)
        stage_weights(d, 1)
        c_min = jnp.min(c_ref[...])
        stage_scores(d + 2, 1)
        stage_values(d, 1)
        stage_weights(d + 1, 0)
        return m + 1, c_min

    def cond(carry):
        m, c_min = carry
        return jnp.logical_and(m < n_pairs, c_min < SKIP_BITS)

    n_pairs = (i * nsub + nsub - 2) // 2
    m_done, _ = lax.while_loop(cond, body, (jnp.int32(0), c_min1))
    stage_values(2 * m_done + 1, 0)
    o_ref[...] = (acc_ref[...] * _silu(gate_ref[...].astype(jnp.float32))).astype(o_ref.dtype)


def _attention(qg, kv, bsz, *, tq=4096, tk=256):
    s = qg.shape[1]
    tile = lambda part: pl.BlockSpec((None, tq, HEAD_DIM),
                                     lambda b, h, i: (b, i, part * N_HEADS + h))
    keys = lambda part: pl.BlockSpec((None, s, HEAD_DIM),
                                     lambda b, h, i: (part * N_HEADS + h, b, 0))
    return pl.pallas_call(
        functools.partial(_attention_kernel, tk=tk),
        grid=(bsz, N_HEADS, s // tq),
        in_specs=[tile(0), keys(0), keys(1), tile(1)],
        out_specs=pl.BlockSpec((None, tq, HEAD_DIM), lambda b, h, i: (b, i, h)),
        out_shape=jax.ShapeDtypeStruct((bsz, s, N_HEADS * HEAD_DIM), jnp.bfloat16),
        scratch_shapes=[
            pltpu.VMEM((tq, HEAD_DIM), jnp.bfloat16),
            pltpu.VMEM((2, tq, tk), jnp.float32),
            pltpu.VMEM((2, tq, tk), jnp.bfloat16),
            pltpu.VMEM((tq, 1), jnp.float32),
            pltpu.VMEM((tq, HEAD_DIM), jnp.float32),
        ],
        compiler_params=_params(("parallel", "parallel", "arbitrary"), 44),
        name="attention",
    )(qg, kv, kv, qg)


def _out_norm_kernel(a_ref, w_ref, x_ref, fn_ref, out_ref):
    y = x_ref[...] + jnp.dot(a_ref[...], w_ref[...], preferred_element_type=jnp.float32)
    ms = jnp.mean(y * y, axis=-1, keepdims=True)
    out_ref[...] = y * lax.rsqrt(ms + EPS) * fn_ref[...]


def _out_norm(a, w, x1, fn, *, tm=512):
    m, d = x1.shape
    rows = pl.BlockSpec((tm, d), lambda i: (i, 0))
    return pl.pallas_call(
        _out_norm_kernel,
        grid=(m // tm,),
        in_specs=[
            pl.BlockSpec((tm, a.shape[1]), lambda i: (i, 0)),
            pl.BlockSpec(w.shape, lambda i: (0, 0), pipeline_mode=pl.Buffered(1)),
            rows,
            pl.BlockSpec((1, d), lambda i: (0, 0)),
        ],
        out_specs=rows,
        out_shape=jax.ShapeDtypeStruct((m, d), jnp.float32),
        compiler_params=_params(("parallel",), 48),
        name="out_norm",
    )(a, w, x1, fn)


def kernel(x, a_norm, a_w_in, a_conv_w, a_conv_b, a_w_r, a_b_r, a_w_i, a_b_i, a_lambda,
           a_w_out, kv_norm, w_kv, b_norm, b_w_in, b_w_out, final_norm):
    bsz, s, d = x.shape
    m = bsz * s
    bf = jnp.bfloat16
    x2 = x.reshape(m, d)

    c = a_w_in.shape[2] // 2
    yg = _lru(x, a_norm[0:1], a_w_in[0].astype(bf), a_conv_w[0], a_conv_b[0:1],
              a_w_r[0].astype(bf), a_b_r[0:1], a_w_i[0].astype(bf), a_b_i[0:1],
              a_lambda[0:1])
    x1, xn = _residual_norm(yg.reshape(m, c), a_w_out[0].astype(bf), x2)
    kv = _matmul_heads(xn, (kv_norm[:, None] * w_kv).astype(bf), bf, name="matmul_kv")
    qg = _matmul(xn, (b_norm[0][:, None] * b_w_in[0]).astype(bf), bf, name="matmul_qg")
    og = _attention(qg.reshape(bsz, s, -1), kv, bsz)
    out = _out_norm(og.reshape(m, -1), b_w_out[0].astype(bf), x1, final_norm.reshape(1, d))
    return out.reshape(bsz, s, d)
```

```python
import functools
import math

import jax
import jax.numpy as jnp
from jax import lax
from jax.experimental import pallas as pl
from jax.experimental.pallas import tpu as pltpu

EPS = 1e-6
LOG2E = 1.0 / math.log(2.0)
LRU_C = 8.0
CONV_W = 4
LRU_BW = 256
N_HEADS = 16
HEAD_DIM = 128
SKIP_BITS = 200.0
DONE_BITS = 1.0e6

SUBLANES = 8
LANES = 128
SEG = SUBLANES
PAD = 4

MIB = 1024 * 1024


def _params(semantics, vmem_mib):
    return pltpu.CompilerParams(dimension_semantics=semantics,
                                vmem_limit_bytes=vmem_mib * MIB)


def _silu(g):
    return g * jax.nn.sigmoid(g)


def _matmul_heads_kernel(a_ref, w_ref, o_ref):
    res = jnp.dot(a_ref[...], w_ref[...], preferred_element_type=jnp.float32)
    for c in range(o_ref.shape[0]):
        o_ref[c] = res[:, c * HEAD_DIM:(c + 1) * HEAD_DIM].astype(o_ref.dtype)


def _matmul_heads(a, w, out_dtype, *, tm=1024, tn=2048, name):
    m, k = a.shape
    n = w.shape[1]
    return pl.pallas_call(
        _matmul_heads_kernel,
        grid=(m // tm, n // tn),
        in_specs=[
            pl.BlockSpec((tm, k), lambda i, j: (i, 0)),
            pl.BlockSpec((k, tn), lambda i, j: (0, j)),
        ],
        out_specs=pl.BlockSpec((tn // HEAD_DIM, tm, HEAD_DIM), lambda i, j: (j, i, 0)),
        out_shape=jax.ShapeDtypeStruct((n // HEAD_DIM, m, HEAD_DIM), out_dtype),
        compiler_params=_params(("parallel", "parallel"), 40),
        name=name,
    )(a, w)


def _lru_kernel(x_ref, g_ref, wx_ref, wg_ref, cw_ref, cb_ref, wr_ref, br_ref, wi_ref, bi_ref,
                lam_ref, o_ref, h_s, g_s, xext, a_s, u_s, hl_s, al_s, y_s, carry_s):
    t = pl.program_id(2)
    tt = x_ref.shape[0]
    cw = wx_ref.shape[1]
    ns = cw // LANES
    seg = tt // SEG
    pitch = seg + PAD

    x = x_ref[...]
    ms = jnp.mean(x * x, axis=-1, keepdims=True)
    h_s[...] = (x * lax.rsqrt(ms + EPS) * g_ref[...]).astype(h_s.dtype)

    @pl.when(t == 0)
    def _():
        xext[:, 0:SUBLANES, :] = jnp.zeros((ns, SUBLANES, LANES), jnp.float32)
        carry_s[...] = jnp.zeros_like(carry_s)

    @pl.when(t > 0)
    def _():
        xext[:, 0:SUBLANES, :] = xext[:, tt:tt + SUBLANES, :]

    def lanes(s):
        return slice(s * LANES, (s + 1) * LANES)

    nblk = cw // LRU_BW

    def project_x(n):
        xb = jnp.dot(h_s[...], wx_ref[:, n * LRU_BW:(n + 1) * LRU_BW],
                     preferred_element_type=jnp.float32)
        for j, s in enumerate(range(n * LRU_BW // LANES, (n + 1) * LRU_BW // LANES)):
            xext[s, SUBLANES:SUBLANES + tt, :] = xb[:, lanes(j)]

    def project_gate(n):
        g_s[:, n * LRU_BW:(n + 1) * LRU_BW] = jnp.dot(
            h_s[...], wg_ref[:, n * LRU_BW:(n + 1) * LRU_BW],
            preferred_element_type=jnp.float32)

    project_x(0)
    for n in range(nblk):
        slabs = range(n * LRU_BW // LANES, (n + 1) * LRU_BW // LANES)
        if n + 1 < nblk:
            project_x(n + 1)
        else:
            project_gate(0)
        parts = []
        for s in slabs:
            xc = cb_ref[:, lanes(s)]
            for w in range(CONV_W):
                off = SUBLANES - (CONV_W - 1) + w
                xc = xc + cw_ref[w:w + 1, lanes(s)] * xext[s, off:off + tt, :]
            parts.append(xc)
        xc = jnp.concatenate(parts, axis=1)
        sl = slice(n * LRU_BW, (n + 1) * LRU_BW)
        xn = xc.astype(jnp.bfloat16)
        r = jax.nn.sigmoid(
            jnp.dot(xn, wr_ref[n], preferred_element_type=jnp.float32) + br_ref[:, sl])
        i = jax.nn.sigmoid(
            jnp.dot(xn, wi_ref[n], preferred_element_type=jnp.float32) + bi_ref[:, sl])
        neg_lam = -lam_ref[:, sl]
        softplus = jnp.maximum(neg_lam, 0.0) + jnp.log1p(jnp.exp(-jnp.abs(neg_lam)))
        neg_log_a = r * (LRU_C * softplus)
        a = jnp.exp2(neg_log_a * (-LOG2E))
        one_minus_a2 = jnp.tanh(neg_log_a) * (a * a + 1.0)
        mult = jnp.where(one_minus_a2 > 0.0, one_minus_a2 * lax.rsqrt(one_minus_a2), 0.0)
        u = mult * (i * xc)
        for j, s in enumerate(slabs):
            for g in range(SEG):
                a_s[s, g * pitch:g * pitch + seg, :] = a[g * seg:(g + 1) * seg, lanes(j)]
                u_s[s, g * pitch:g * pitch + seg, :] = u[g * seg:(g + 1) * seg, lanes(j)]

    for n in range(1, nblk):
        project_gate(n)

    for s in range(ns):
        h = jnp.zeros((SEG, LANES), jnp.float32)
        prod = jnp.ones((SEG, LANES), jnp.float32)
        for p in range(seg):
            ap = a_s[s, pl.ds(p, SEG, stride=pitch), :]
            h = ap * h + u_s[s, pl.ds(p, SEG, stride=pitch), :]
            prod = ap * prod
            hl_s[s, p] = h
            al_s[s, p] = prod
        entering = [carry_s[s]]
        for g in range(SEG - 1):
            entering.append(prod[g:g + 1] * entering[-1] + h[g:g + 1])
        carry_s[s] = prod[SEG - 1:SEG] * entering[-1] + h[SEG - 1:SEG]
        h_in = jnp.concatenate(entering, axis=0)
        for p in range(seg):
            y_s[s, pl.ds(p, SEG, stride=pitch), :] = hl_s[s, p] + al_s[s, p] * h_in

    for s in range(ns):
        for g in range(SEG):
            rows = slice(g * seg, (g + 1) * seg)
            y = y_s[s, g * pitch:g * pitch + seg, :]
            o_ref[rows, lanes(s)] = (y * _silu(g_s[rows, lanes(s)])).astype(o_ref.dtype)


def _lru(x, g, w_in, conv_w, conv_b, w_r, b_r, w_i, b_i, lam, *, tt=256, cw=2560):
    bsz, s, d = x.shape
    c = w_in.shape[1] // 2
    nc = c // cw
    nb = cw // LRU_BW
    ns = cw // LANES
    seg = tt // SEG
    row = lambda b, ci, t: (0, ci)
    return pl.pallas_call(
        _lru_kernel,
        grid=(bsz, nc, s // tt),
        in_specs=[
            pl.BlockSpec((None, tt, d), lambda b, ci, t: (b, t, 0)),
            pl.BlockSpec((1, d), lambda b, ci, t: (0, 0)),
            pl.BlockSpec((d, cw), lambda b, ci, t: (0, ci), pipeline_mode=pl.Buffered(1)),
            pl.BlockSpec((d, cw), lambda b, ci, t: (0, nc + ci), pipeline_mode=pl.Buffered(1)),
            pl.BlockSpec((CONV_W, cw), row),
            pl.BlockSpec((1, cw), row),
            pl.BlockSpec((nb, LRU_BW, LRU_BW), lambda b, ci, t: (ci, 0, 0)),
            pl.BlockSpec((1, cw), row),
            pl.BlockSpec((nb, LRU_BW, LRU_BW), lambda b, ci, t: (ci, 0, 0)),
            pl.BlockSpec((1, cw), row),
            pl.BlockSpec((1, cw), row),
        ],
        out_specs=pl.BlockSpec((None, tt, cw), lambda b, ci, t: (b, t, ci)),
        out_shape=jax.ShapeDtypeStruct((bsz, s, c), jnp.bfloat16),
        scratch_shapes=[
            pltpu.VMEM((tt, d), jnp.bfloat16),
            pltpu.VMEM((tt, cw), jnp.float32),
            pltpu.VMEM((ns, tt + SUBLANES, LANES), jnp.float32),
            pltpu.VMEM((ns, SEG * (seg + PAD), LANES), jnp.float32),
            pltpu.VMEM((ns, SEG * (seg + PAD), LANES), jnp.float32),
            pltpu.VMEM((ns, seg, SEG, LANES), jnp.float32),
            pltpu.VMEM((ns, seg, SEG, LANES), jnp.float32),
            pltpu.VMEM((ns, SEG * (seg + PAD), LANES), jnp.float32),
            pltpu.VMEM((ns, 1, LANES), jnp.float32),
        ],
        compiler_params=_params(("parallel", "parallel", "arbitrary"), 52),
        name="lru",
    )(x, g, w_in, w_in, conv_w, conv_b, w_r, b_r, w_i, b_i, lam)


def _residual_norm_kernel(a_ref, w_ref, r_ref, x1_ref, xn_ref):
    x1 = r_ref[...] + jnp.dot(a_ref[...], w_ref[...], preferred_element_type=jnp.float32)
    x1_ref[...] = x1
    ms = jnp.mean(x1 * x1, axis=-1, keepdims=True)
    xn_ref[...] = (x1 * lax.rsqrt(ms + EPS)).astype(xn_ref.dtype)


def _residual_norm(a, w, res, *, tm=512):
    m, k = a.shape
    d = w.shape[1]
    rows = pl.BlockSpec((tm, d), lambda i: (i, 0))
    return pl.pallas_call(
        _residual_norm_kernel,
        grid=(m // tm,),
        in_specs=[
            pl.BlockSpec((tm, k), lambda i: (i, 0)),
            pl.BlockSpec((k, d), lambda i: (0, 0), pipeline_mode=pl.Buffered(1)),
            rows,
        ],
        out_specs=[rows, rows],
        out_shape=[jax.ShapeDtypeStruct((m, d), jnp.float32),
                   jax.ShapeDtypeStruct((m, d), jnp.bfloat16)],
        compiler_params=_params(("parallel",), 48),
        name="residual_norm",
    )(a, w, res)


def _attention_kernel(q_ref, k_ref, v_ref, gate_ref, o_ref, q_s, z_s, w_s, c_ref, acc_ref, *, tk):
    i = pl.program_id(2)
    tq, dh = q_ref.shape
    nsub = tq // tk
    q_s[...] = (q_ref[...].astype(jnp.float32) * (LOG2E / math.sqrt(dh))).astype(jnp.bfloat16)

    rows = lax.broadcasted_iota(jnp.int32, (tk, tk), 0)
    cols = lax.broadcasted_iota(jnp.int32, (tk, tk), 1)
    tri = (rows >= cols).astype(jnp.bfloat16)

    c_ref[...] = jnp.zeros_like(c_ref)
    acc_ref[...] = jnp.zeros_like(acc_ref)

    def sub(a):
        return pl.ds(a * tk, tk)

    def key_block(d, a):
        blk = i * nsub + a - d
        return pl.ds(pl.multiple_of(jnp.maximum(blk, 0) * tk, tk), tk), blk >= 0

    def scores(q, kb):
        return lax.dot_general(q, kb, (((1,), (1,)), ((), ())),
                               preferred_element_type=jnp.float32)

    def weights(z, c, mask):
        sp = jnp.maximum(z, 0.0) + jnp.log(1.0 + jnp.exp2(-jnp.abs(z))) * LOG2E
        if mask is not None:
            sp = jnp.where(mask, sp, 0.0)
        incl = jnp.dot(sp.astype(jnp.bfloat16), tri, preferred_element_type=jnp.float32)
        w = jnp.exp2(z - incl - c)
        if mask is not None:
            w = jnp.where(mask, w, 0.0)
        return w.astype(jnp.bfloat16), incl[:, 0:1]

    def stage_scores(d, par):
        for a in range(nsub):
            keys, _ = key_block(d, a)
            z_s[par, sub(a), :] = scores(q_s[sub(a), :], k_ref[keys, :])

    def stage_weights(d, par, diagonal=False):
        if diagonal:
            row = lax.broadcasted_iota(jnp.int32, (tq, tk), 0) & (tk - 1)
            mask = lax.broadcasted_iota(jnp.int32, (tq, tk), 1) < row
        else:
            mask = None
            for a in range(nsub):
                _, exists = key_block(d, a)
                c_ref[sub(a), :] += jnp.where(exists, 0.0, DONE_BITS)
        w, total = weights(z_s[par], c_ref[...], mask)
        w_s[par] = w
        c_ref[...] += total

    def stage_values(d, par):
        for a in range(nsub):
            keys, _ = key_block(d, a)
            acc_ref[sub(a), :] += jnp.dot(w_s[par, sub(a), :], v_ref[keys, :],
                                          preferred_element_type=jnp.float32)

    stage_scores(0, 1)
    stage_scores(1, 0)
    stage_weights(0, 1, diagonal=True)
    stage_values(0, 1)
    stage_weights(1, 0)
    c_min1 = jnp.min(c_ref[...])

    @pl.when(c_min1 < SKIP_BITS)
    def _():
        stage_scores(2, 1)

    def body(carry):
        m, _ = carry
        d = 2 * m + 2
        stage_scores(d + 1, 0)
        stage_values(d - 1, 0)
        stage_weights(d, 1)
        c_min = jnp.min(c_ref[...])
        stage_scores(d + 2, 1)
        stage_values(d, 1)
        stage_weights(d + 1, 0)
        return m + 1, c_min

    def cond(carry):
        m, c_min = carry
        return jnp.logical_and(m < n_pairs, c_min < SKIP_BITS)

    n_pairs = (i * nsub + nsub - 2) // 2
    m_done, _ = lax.while_loop(cond, body, (jnp.int32(0), c_min1))
    stage_values(2 * m_done + 1, 0)
    o_ref[...] = (acc_ref[...] * _silu(gate_ref[...].astype(jnp.float32))).astype(o_ref.dtype)


def _attention(qg, kv, bsz, *, tq=4096, tk=256):
    s = qg.shape[1] // bsz
    nq = s // tq
    tile = lambda part: pl.BlockSpec((None, tq, HEAD_DIM),
                                     lambda b, h, i: (part * N_HEADS + h, b * nq + i, 0))
    keys = lambda part: pl.BlockSpec((None, s, HEAD_DIM),
                                     lambda b, h, i: (part * N_HEADS + h, b, 0))
    return pl.pallas_call(
        functools.partial(_attention_kernel, tk=tk),
        grid=(bsz, N_HEADS, s // tq),
        in_specs=[tile(0), keys(0), keys(1), tile(1)],
        out_specs=pl.BlockSpec((None, tq, HEAD_DIM), lambda b, h, i: (b, i, h)),
        out_shape=jax.ShapeDtypeStruct((bsz, s, N_HEADS * HEAD_DIM), jnp.bfloat16),
        scratch_shapes=[
            pltpu.VMEM((tq, HEAD_DIM), jnp.bfloat16),
            pltpu.VMEM((2, tq, tk), jnp.float32),
            pltpu.VMEM((2, tq, tk), jnp.bfloat16),
            pltpu.VMEM((tq, 1), jnp.float32),
            pltpu.VMEM((tq, HEAD_DIM), jnp.float32),
        ],
        compiler_params=_params(("parallel", "parallel", "arbitrary"), 44),
        name="attention",
    )(qg, kv, kv, qg)


def _out_norm_kernel(a_ref, w_ref, x_ref, fn_ref, out_ref):
    y = x_ref[...] + jnp.dot(a_ref[...], w_ref[...], preferred_element_type=jnp.float32)
    ms = jnp.mean(y * y, axis=-1, keepdims=True)
    out_ref[...] = y * lax.rsqrt(ms + EPS) * fn_ref[...]


def _out_norm(a, w, x1, fn, *, tm=512):
    m, d = x1.shape
    rows = pl.BlockSpec((tm, d), lambda i: (i, 0))
    return pl.pallas_call(
        _out_norm_kernel,
        grid=(m // tm,),
        in_specs=[
            pl.BlockSpec((tm, a.shape[1]), lambda i: (i, 0)),
            pl.BlockSpec(w.shape, lambda i: (0, 0), pipeline_mode=pl.Buffered(1)),
            rows,
            pl.BlockSpec((1, d), lambda i: (0, 0)),
        ],
        out_specs=rows,
        out_shape=jax.ShapeDtypeStruct((m, d), jnp.float32),
        compiler_params=_params(("parallel",), 48),
        name="out_norm",
    )(a, w, x1, fn)


def kernel(x, a_norm, a_w_in, a_conv_w, a_conv_b, a_w_r, a_b_r, a_w_i, a_b_i, a_lambda,
           a_w_out, kv_norm, w_kv, b_norm, b_w_in, b_w_out, final_norm):
    bsz, s, d = x.shape
    m = bsz * s
    bf = jnp.bfloat16
    x2 = x.reshape(m, d)

    c = a_w_in.shape[2] // 2
    yg = _lru(x, a_norm[0:1], a_w_in[0].astype(bf), a_conv_w[0], a_conv_b[0:1],
              a_w_r[0].astype(bf), a_b_r[0:1], a_w_i[0].astype(bf), a_b_i[0:1],
              a_lambda[0:1])
    x1, xn = _residual_norm(yg.reshape(m, c), a_w_out[0].astype(bf), x2)
    kv = _matmul_heads(xn, (kv_norm[:, None] * w_kv).astype(bf), bf, name="matmul_kv")
    qg = _matmul_heads(xn, (b_norm[0][:, None] * b_w_in[0]).astype(bf), bf, name="matmul_qg")
    og = _attention(qg, kv, bsz)
    out = _out_norm(og.reshape(m, -1), b_w_out[0].astype(bf), x1, final_norm.reshape(1, d))
    return out.reshape(bsz, s, d)
```
